```python
import jax, jax.numpy as jnp
from jax import lax
import numpy as np


D_MODEL = 1024
BATCH = 8
SEQ = 4096
DEPTH = 4

D_PLE = 256
HEAD_DIM = 64
CONV_CH = 256
CONV_K = 31
NSA_HEADS = 8
NSA_GROUPS = 2
NSA_HPG = NSA_HEADS // NSA_GROUPS
NSA_CMP_STRIDE = 16
NSA_CMP_LEN = 2 * NSA_CMP_STRIDE
NSA_CMP_HIDDEN = 128
NSA_SEL_LEN = 64
NSA_TOP = 16
NSA_WINDOW = 512
NSA_Q_BLOCK = 64
HGRN_HEADS = 4
HGRN_W = HGRN_HEADS * HEAD_DIM
HGRN_CHUNK = 64
D_FF = 2816
FFN_CONV_K = 3
MIX_W = CONV_CH + NSA_HEADS * HEAD_DIM + HGRN_W
CONV_COLS = 2 * CONV_CH
NSA_Q_COLS = NSA_HEADS * HEAD_DIM
NSA_KV_COLS = 6 * NSA_GROUPS * HEAD_DIM
NSA_GATE_COLS = 3 * NSA_HEADS
HGRN_COLS = 4 * HGRN_W
IN_COLS = CONV_COLS + NSA_Q_COLS + NSA_KV_COLS + NSA_GATE_COLS + HGRN_COLS
IN_SPLITS = (CONV_COLS, CONV_COLS + NSA_Q_COLS, CONV_COLS + NSA_Q_COLS + NSA_KV_COLS, CONV_COLS + NSA_Q_COLS + NSA_KV_COLS + NSA_GATE_COLS)
DN_ALPHA = (2 * DEPTH) ** 0.25
DN_BETA = (8 * DEPTH) ** -0.25
ATTN_SCALE = HEAD_DIM ** -0.5
LN_EPS = 1e-5
MASK_VALUE = -1e30
FORCE_SCORE = 1e9

kernel_name = 'hybrid_conv_nsa_hgrn2_deepnorm'


def layer_norm(x, g, b):
    xf = x.astype(jnp.float32)
    mu = jnp.mean(xf, axis=-1, keepdims=True)
    var = jnp.mean(jnp.square(xf - mu), axis=-1, keepdims=True)
    return ((xf - mu) * lax.rsqrt(var + LN_EPS) * g + b).astype(x.dtype)


def causal_dwconv(x, w, b):
    k = w.shape[0]
    y = lax.conv_general_dilated(x, w[:, None, :], window_strides=(1,), padding=[(k - 1, 0)],
                                 dimension_numbers=('NWC', 'WIO', 'NWC'), feature_group_count=x.shape[-1])
    return y + b


def masked_softmax(s, mask):
    p = jax.nn.softmax(jnp.where(mask, s, MASK_VALUE), axis=-1)
    return jnp.where(mask, p, 0.0)


def conv_module(u, conv_w, conv_b, ln_g, ln_b):
    a, gate = jnp.split(u, 2, axis=-1)
    glu = a * jax.nn.sigmoid(gate)
    c = causal_dwconv(glu, conv_w, conv_b)
    return jax.nn.silu(layer_norm(c, ln_g, ln_b))


def nsa_compress(kv, pe, w1, w2):
    b, s, g, dh = kv.shape
    ch = kv.reshape(b, s // NSA_CMP_STRIDE, NSA_CMP_STRIDE, g, dh)
    blocks = jnp.concatenate([ch[:, :-1], ch[:, 1:]], axis=2) + pe[None, None, :, None, :]
    n_cmp = blocks.shape[1]
    flat = blocks.transpose(0, 1, 3, 2, 4).reshape(b, n_cmp, g, NSA_CMP_LEN * dh)
    return jax.nn.gelu(flat @ w1) @ w2


def nsa_mixer(q, k_cmp, v_cmp, k_slc, v_slc, k_win, v_win, gate_logits, pe_k, pe_v, w1_k, w2_k, w1_v, w2_v):
    f32 = jnp.float32
    b, s = q.shape[0], q.shape[1]
    n_cmp = s // NSA_CMP_STRIDE - 1
    n_sel = s // NSA_SEL_LEN
    n_top = min(NSA_TOP, n_sel)
    kc = nsa_compress(k_cmp, pe_k, w1_k, w2_k)
    vc = nsa_compress(v_cmp, pe_v, w1_v, w2_v).astype(f32)
    cmp_start = jnp.arange(n_cmp) * NSA_CMP_STRIDE
    cmp_end = cmp_start + NSA_CMP_LEN - 1
    sel_start = jnp.arange(n_sel) * NSA_SEL_LEN
    cmp_to_sel = ((cmp_start[:, None] <= sel_start[None, :] + NSA_SEL_LEN - 1)
                  & (cmp_end[:, None] >= sel_start[None, :])).astype(f32)
    k_blocks = k_slc.reshape(b, n_sel, NSA_SEL_LEN, NSA_GROUPS, HEAD_DIM).transpose(0, 3, 1, 2, 4)
    v_blocks = v_slc.reshape(b, n_sel, NSA_SEL_LEN, NSA_GROUPS, HEAD_DIM).transpose(0, 3, 1, 2, 4)
    pad = ((0, 0), (NSA_WINDOW, 0), (0, 0), (0, 0))
    k_win_p = jnp.pad(k_win, pad)
    v_win_p = jnp.pad(v_win, pad)
    qg = q.reshape(b, s, NSA_GROUPS, NSA_HPG, HEAD_DIM)
    gates = jax.nn.sigmoid(gate_logits.astype(f32)).reshape(b, s, NSA_GROUPS, NSA_HPG, 3)
    gather = jax.vmap(jax.vmap(lambda blocks, idx: blocks[idx]))
    sel_off = jnp.arange(NSA_SEL_LEN)
    win_off = jnp.arange(NSA_WINDOW + NSA_Q_BLOCK)
    j_sel = jnp.arange(n_sel)

    def block_fn(c):
        t0 = c * NSA_Q_BLOCK
        qb = lax.dynamic_slice_in_dim(qg, t0, NSA_Q_BLOCK, axis=1)
        pos = t0 + jnp.arange(NSA_Q_BLOCK)
        sc = jnp.einsum('bqghd,bngd->bghqn', qb, kc).astype(f32) * ATTN_SCALE
        p_cmp = masked_softmax(sc, cmp_end[None, :] <= pos[:, None])
        o_cmp = jnp.einsum('bghqn,bngd->bqghd', p_cmp, vc)
        imp = jnp.einsum('bghqn,nj->bgqj', p_cmp, cmp_to_sel)
        blk = pos // NSA_SEL_LEN
        forced = (j_sel[None, :] == 0) | (j_sel[None, :] == blk[:, None]) | (j_sel[None, :] == blk[:, None] - 1)
        causal = j_sel[None, :] <= blk[:, None]
        score = jnp.where(causal, jnp.where(forced, FORCE_SCORE, imp), -jnp.inf)
        top_s, top_i = lax.top_k(score, n_top)
        kg = gather(k_blocks, top_i)
        vg = gather(v_blocks, top_i).astype(f32)
        key_pos = top_i[..., None] * NSA_SEL_LEN + sel_off
        sel_mask = jnp.isfinite(top_s)[..., None] & (key_pos <= pos[None, None, :, None, None])
        ss = jnp.einsum('bqghd,bgqnld->bghqnl', qb, kg).astype(f32) * ATTN_SCALE
        ss = ss.reshape(b, NSA_GROUPS, NSA_HPG, NSA_Q_BLOCK, n_top * NSA_SEL_LEN)
        p_slc = masked_softmax(ss, sel_mask.reshape(b, NSA_GROUPS, 1, NSA_Q_BLOCK, n_top * NSA_SEL_LEN))
        p_slc = p_slc.reshape(b, NSA_GROUPS, NSA_HPG, NSA_Q_BLOCK, n_top, NSA_SEL_LEN)
        o_slc = jnp.einsum('bghqnl,bgqnld->bqghd', p_slc, vg)
        kw = lax.dynamic_slice_in_dim(k_win_p, t0, NSA_WINDOW + NSA_Q_BLOCK, axis=1)
        vw = lax.dynamic_slice_in_dim(v_win_p, t0, NSA_WINDOW + NSA_Q_BLOCK, axis=1).astype(f32)
        kpos = t0 - NSA_WINDOW + win_off
        wmask = ((kpos[None, :] <= pos[:, None]) & (pos[:, None] - kpos[None, :] < NSA_WINDOW)
                 & (kpos[None, :] >= 0))
        sw = jnp.einsum('bqghd,bkgd->bghqk', qb, kw).astype(f32) * ATTN_SCALE
        p_win = masked_softmax(sw, wmask)
        o_win = jnp.einsum('bghqk,bkgd->bqghd', p_win, vw)
        gb = lax.dynamic_slice_in_dim(gates, t0, NSA_Q_BLOCK, axis=1)
        return gb[..., 0:1] * o_cmp + gb[..., 1:2] * o_slc + gb[..., 2:3] * o_win

    out = lax.map(block_fn, jnp.arange(s // NSA_Q_BLOCK))
    return jnp.moveaxis(out, 0, 1).reshape(b, s, NSA_HEADS * HEAD_DIM).astype(q.dtype)


def hgrn2_mixer(u, lb, norm_g):
    f32 = jnp.float32
    b, s, _ = u.shape
    q, fz, i_in, g = jnp.split(u, 4, axis=-1)
    f = lb + (1.0 - lb) * jax.nn.sigmoid(fz.astype(f32))
    log_f = jnp.log(f)
    k = 1.0 - f
    n_chunks = s // HGRN_CHUNK

    def to_chunks(a):
        return a.astype(f32).reshape(b, n_chunks, HGRN_CHUNK, HGRN_HEADS, HEAD_DIM).transpose(1, 0, 3, 2, 4)

    tri = jnp.tril(jnp.ones((HGRN_CHUNK, HGRN_CHUNK), dtype=bool))

    def step(state, inp):
        qc, lfc, kc, vc = inp
        a = jnp.cumsum(lfc, axis=2)
        rel = a[:, :, :, None, :] - a[:, :, None, :, :]
        decay = jnp.exp(jnp.where(tri[:, :, None], rel, -jnp.inf))
        scores = jnp.einsum('bhtk,bhsk,bhtsk->bhts', qc, kc, decay)
        o = (jnp.einsum('bhts,bhsv->bhtv', scores, vc)
             + jnp.einsum('bhtk,bhkv->bhtv', qc * jnp.exp(a), state))
        a_last = a[:, :, -1:, :]
        new_state = (jnp.exp(a_last[:, :, 0, :])[..., None] * state
                     + jnp.einsum('bhsk,bhsv->bhkv', kc * jnp.exp(a_last - a), vc))
        return new_state, o

    state0 = jnp.zeros((b, HGRN_HEADS, HEAD_DIM, HEAD_DIM), f32)
    _, o = lax.scan(step, state0, (to_chunks(q), to_chunks(log_f), to_chunks(k), to_chunks(i_in)))
    o = o.transpose(1, 0, 3, 2, 4).reshape(b, s, HGRN_HEADS, HEAD_DIM)
    o = o * lax.rsqrt(jnp.mean(jnp.square(o), axis=-1, keepdims=True) + LN_EPS) * norm_g
    o = o * jax.nn.silu(g.reshape(b, s, HGRN_HEADS, HEAD_DIM).astype(f32))
    return o.reshape(b, s, HGRN_W).astype(u.dtype)


def hybrid_mixer(h, w_in, conv_w, conv_b, conv_ln_g, conv_ln_b, cmp_pe_k, cmp_pe_v, cmp_w1_k, cmp_w2_k,
                 cmp_w1_v, cmp_w2_v, lb, hgrn_norm_g, w_out):
    b, s, _ = h.shape
    u = h @ w_in
    u_conv, u_q, u_kv, u_gate, u_hgrn = jnp.split(u, IN_SPLITS, axis=-1)
    y_conv = conv_module(u_conv, conv_w, conv_b, conv_ln_g, conv_ln_b)
    q = u_q.reshape(b, s, NSA_HEADS, HEAD_DIM)
    kv = u_kv.reshape(b, s, 6, NSA_GROUPS, HEAD_DIM)
    gate_logits = u_gate.reshape(b, s, NSA_HEADS, 3)
    y_nsa = nsa_mixer(q, kv[:, :, 0], kv[:, :, 1], kv[:, :, 2], kv[:, :, 3], kv[:, :, 4], kv[:, :, 5],
                      gate_logits, cmp_pe_k, cmp_pe_v, cmp_w1_k, cmp_w2_k, cmp_w1_v, cmp_w2_v)
    y_hgrn = hgrn2_mixer(u_hgrn, lb, hgrn_norm_g)
    return jnp.concatenate([y_conv, y_nsa, y_hgrn], axis=-1) @ w_out


def conv_ffn(h, w_up, conv_w, conv_b, w_down):
    u = causal_dwconv(h @ w_up, conv_w, conv_b)
    val, gate = jnp.split(u, 2, axis=-1)
    return (jax.nn.silu(gate) * val) @ w_down


def setup_inputs(seed: int = 0) -> dict:
    key = jax.random.key(seed)
    ks = jax.random.split(key, 26)

    def nrm(k, shape, scale):
        return jax.random.normal(k, shape, jnp.float32) * scale

    return {
        'x': nrm(ks[0], (BATCH, SEQ, D_MODEL), 1.0),
        'p': nrm(ks[1], (DEPTH, BATCH, SEQ, D_PLE), 1.0),
        'w_in': nrm(ks[2], (DEPTH, D_MODEL, IN_COLS), D_MODEL ** -0.5),
        'conv_w': nrm(ks[3], (DEPTH, CONV_K, CONV_CH), CONV_K ** -0.5),
        'conv_b': nrm(ks[4], (DEPTH, CONV_CH), 0.02),
        'conv_ln_g': 1.0 + nrm(ks[5], (DEPTH, CONV_CH), 0.02),
        'conv_ln_b': nrm(ks[6], (DEPTH, CONV_CH), 0.02),
        'cmp_pe_k': nrm(ks[7], (DEPTH, NSA_CMP_LEN, HEAD_DIM), 0.02),
        'cmp_pe_v': nrm(ks[8], (DEPTH, NSA_CMP_LEN, HEAD_DIM), 0.02),
        'cmp_w1_k': nrm(ks[9], (DEPTH, NSA_CMP_LEN * HEAD_DIM, NSA_CMP_HIDDEN), (NSA_CMP_LEN * HEAD_DIM) ** -0.5),
        'cmp_w2_k': nrm(ks[10], (DEPTH, NSA_CMP_HIDDEN, HEAD_DIM), NSA_CMP_HIDDEN ** -0.5),
        'cmp_w1_v': nrm(ks[11], (DEPTH, NSA_CMP_LEN * HEAD_DIM, NSA_CMP_HIDDEN), (NSA_CMP_LEN * HEAD_DIM) ** -0.5),
        'cmp_w2_v': nrm(ks[12], (DEPTH, NSA_CMP_HIDDEN, HEAD_DIM), NSA_CMP_HIDDEN ** -0.5),
        'lb_logits': nrm(ks[13], (DEPTH, HGRN_W), 0.1),
        'hgrn_norm_g': 1.0 + nrm(ks[14], (DEPTH, HEAD_DIM), 0.02),
        'w_out': nrm(ks[15], (DEPTH, MIX_W, D_MODEL), MIX_W ** -0.5 * DN_BETA),
        'ln1_g': 1.0 + nrm(ks[16], (DEPTH, D_MODEL), 0.02),
        'ln1_b': nrm(ks[17], (DEPTH, D_MODEL), 0.02),
        'w_up': nrm(ks[18], (DEPTH, D_MODEL, 2 * D_FF), D_MODEL ** -0.5),
        'ffn_conv_w': nrm(ks[19], (DEPTH, FFN_CONV_K, 2 * D_FF), FFN_CONV_K ** -0.5),
        'ffn_conv_b': nrm(ks[20], (DEPTH, 2 * D_FF), 0.02),
        'w_down': nrm(ks[21], (DEPTH, D_FF, D_MODEL), D_FF ** -0.5 * DN_BETA),
        'w_ple_gate': nrm(ks[22], (DEPTH, D_MODEL, D_MODEL), D_MODEL ** -0.5),
        'w_ple_proj': nrm(ks[23], (DEPTH, D_PLE, D_MODEL), D_PLE ** -0.5 * DN_BETA),
        'ln2_g': 1.0 + nrm(ks[24], (DEPTH, D_MODEL), 0.02),
        'ln2_b': nrm(ks[25], (DEPTH, D_MODEL), 0.02),
    }


def reference(x, p, w_in, conv_w, conv_b, conv_ln_g, conv_ln_b, cmp_pe_k, cmp_pe_v, cmp_w1_k, cmp_w2_k,
              cmp_w1_v, cmp_w2_v, lb_logits, hgrn_norm_g, w_out, ln1_g, ln1_b, w_up, ffn_conv_w, ffn_conv_b,
              w_down, w_ple_gate, w_ple_proj, ln2_g, ln2_b):
    probs = jax.nn.softmax(lb_logits.astype(jnp.float32), axis=0)
    lbs = jnp.cumsum(probs, axis=0) - probs[0]
    h = x
    for i in range(DEPTH):
        m = hybrid_mixer(h, w_in[i], conv_w[i], conv_b[i], conv_ln_g[i], conv_ln_b[i], cmp_pe_k[i], cmp_pe_v[i],
                         cmp_w1_k[i], cmp_w2_k[i], cmp_w1_v[i], cmp_w2_v[i], lbs[i], hgrn_norm_g[i], w_out[i])
        h = layer_norm(DN_ALPHA * h + m, ln1_g[i], ln1_b[i])
        f = conv_ffn(h, w_up[i], ffn_conv_w[i], ffn_conv_b[i], w_down[i])
        e = jax.nn.sigmoid(h @ w_ple_gate[i]) * (p[i] @ w_ple_proj[i])
        h = layer_norm(DN_ALPHA * h + f + e, ln2_g[i], ln2_b[i])
    return h
```

```python
import functools

import numpy as np
import jax
import jax.numpy as jnp
from jax import lax
from jax.experimental import pallas as pl
from jax.experimental.pallas import tpu as pltpu

F32 = jnp.float32
BF16 = jnp.bfloat16

D_MODEL = 1024
DEPTH = 4
D_PLE = 256
HEAD_DIM = 64
CONV_CH = 256
CONV_K = 31
NSA_HEADS = 8
NSA_GROUPS = 2
NSA_HPG = NSA_HEADS // NSA_GROUPS
NSA_CMP_STRIDE = 16
NSA_CMP_LEN = 32
NSA_CMP_HIDDEN = 128
NSA_SEL_LEN = 64
NSA_TOP = 16
NSA_WINDOW = 512
HGRN_HEADS = 4
HGRN_W = HGRN_HEADS * HEAD_DIM
D_FF = 2816
DN_ALPHA = (2 * DEPTH) ** 0.25
ATTN_SCALE = HEAD_DIM ** -0.5
LN_EPS = 1e-5
MASK_VALUE = -1e30
FORCE_SCORE = 1e9

LANES = 128
VMEM_LIMIT = 56 * 1024 * 1024

_O_CONV, _O_Q, _O_KC, _O_VC, _O_KV4, _O_GATE, _O_HG, _O_END = 0, 512, 1024, 1152, 1280, 1792, 1920, 2944

_NT = (((1,), (1,)), ((), ()))
_TN = (((0,), (0,)), ((), ()))


def _cparams(sem):
    return pltpu.CompilerParams(dimension_semantics=sem, vmem_limit_bytes=VMEM_LIMIT)


def _dot(a, b):
    return jnp.dot(a, b, preferred_element_type=F32)


def _dot_nt(a, b):
    return lax.dot_general(a, b, _NT, preferred_element_type=F32)


def _split_dot(x, w):
    hi = x.astype(BF16)
    lo = (x - hi.astype(F32)).astype(BF16)
    return _dot(hi, w) + _dot(lo, w)


def _sigmoid(x):
    return 1.0 / (1.0 + jnp.exp(-x))


def _layer_norm(z, g, b):
    mu = jnp.mean(z, axis=-1, keepdims=True)
    zc = z - mu
    var = jnp.mean(zc * zc, axis=-1, keepdims=True)
    return zc * lax.rsqrt(var + LN_EPS) * g + b


def _inproj_kernel(h_ref, w_ref, conv_ref, q_ref, kc_ref, vc_ref, kv4_ref, gate_ref, hg_ref):
    x = h_ref[...].astype(BF16)
    conv_ref[...] = _dot(x, w_ref[:, _O_CONV:_O_Q])
    q_ref[...] = _dot(x, w_ref[:, _O_Q:_O_KC]).astype(BF16)
    kc_ref[...] = _dot(x, w_ref[:, _O_KC:_O_VC]).astype(BF16)
    vc_ref[...] = _dot(x, w_ref[:, _O_VC:_O_KV4]).astype(BF16)
    kv4_ref[...] = _dot(x, w_ref[:, _O_KV4:_O_GATE]).astype(BF16)
    gate_ref[...] = _dot(x, w_ref[:, _O_GATE:_O_HG])
    hg_ref[...] = _dot(x, w_ref[:, _O_HG:_O_END])


def _inproj(h, w, tm=512):
    t = h.shape[0]
    widths = (512, 512, 128, 128, 512, 128, 1024)
    dtypes = (F32, BF16, BF16, BF16, BF16, F32, F32)
    return pl.pallas_call(
        _inproj_kernel,
        grid=(t // tm,),
        in_specs=[pl.BlockSpec((tm, D_MODEL), lambda i: (i, 0)),
                  pl.BlockSpec((D_MODEL, _O_END), lambda i: (0, 0))],
        out_specs=[pl.BlockSpec((tm, wd), lambda i: (i, 0)) for wd in widths],
        out_shape=[jax.ShapeDtypeStruct((t, wd), dt) for wd, dt in zip(widths, dtypes)],
        compiler_params=_cparams(("parallel",)),
        name="inproj",
    )(h, w)


_CONV_HALO = 32
_CONV_SUB = 64


def _convmod_kernel(u_ref, w_ref, b_ref, g_ref, beta_ref, o_ref, buf_ref, *, tt):
    @pl.when(pl.program_id(1) == 0)
    def _():
        buf_ref[0:_CONV_HALO, :] = jnp.zeros((_CONV_HALO, CONV_CH), F32)

    u = u_ref[0]
    buf_ref[_CONV_HALO:_CONV_HALO + tt, :] = u[:, :CONV_CH] * _sigmoid(u[:, CONV_CH:])
    w = w_ref[...]
    base = _CONV_HALO - (CONV_K - 1)
    for r in range(tt // _CONV_SUB):
        acc = jnp.broadcast_to(b_ref[...], (_CONV_SUB, CONV_CH))
        for k in range(CONV_K):
            s0 = r * _CONV_SUB + base + k
            acc = acc + w[k:k + 1, :] * buf_ref[s0:s0 + _CONV_SUB, :]
        y = _layer_norm(acc, g_ref[...], beta_ref[...])
        o_ref[0, r * _CONV_SUB:(r + 1) * _CONV_SUB, :] = (y * _sigmoid(y)).astype(BF16)
    buf_ref[0:_CONV_HALO, :] = buf_ref[tt:tt + _CONV_HALO, :]


def _convmod(u, w, b, g, beta, tt=512):
    bsz, s, _ = u.shape
    vec = pl.BlockSpec((1, CONV_CH), lambda bi, ti: (0, 0))
    return pl.pallas_call(
        functools.partial(_convmod_kernel, tt=tt),
        grid=(bsz, s // tt),
        in_specs=[pl.BlockSpec((1, tt, 2 * CONV_CH), lambda bi, ti: (bi, ti, 0)),
                  pl.BlockSpec((CONV_K + 1, CONV_CH), lambda bi, ti: (0, 0)), vec, vec, vec],
        out_specs=pl.BlockSpec((1, tt, CONV_CH), lambda bi, ti: (bi, ti, 0)),
        out_shape=jax.ShapeDtypeStruct((bsz, s, CONV_CH), BF16),
        scratch_shapes=[pltpu.VMEM((tt + _CONV_HALO, CONV_CH), F32)],
        compiler_params=_cparams(("parallel", "arbitrary")),
        name="convmod",
    )(u, w, b, g, beta)


def _gelu_tanh(x):
    return 0.5 * x * (1.0 + jnp.tanh(0.7978845608028654 * (x + 0.044715 * (x * x * x))))


def _compress_kernel(x_ref, pe_ref, wt_ref, wb_ref, w2_ref, o_ref):
    x = x_ref[0].astype(F32)
    n = x.shape[0]
    top = _dot((x + pe_ref[0:1, :]).astype(BF16), wt_ref[...])
    bot = _dot((x + pe_ref[1:2, :]).astype(BF16), wb_ref[...])
    hid = top + pltpu.roll(bot, n - 1, 0)
    o_ref[0] = _dot(_gelu_tanh(hid).astype(BF16), w2_ref[...]).astype(BF16)


def _compress(xr, pe2, wt, wb, w2):
    bsz, n, width = xr.shape
    hid = wt.shape[1]
    full = lambda shape: pl.BlockSpec(shape, lambda bi: (0,) * len(shape))
    return pl.pallas_call(
        _compress_kernel,
        grid=(bsz,),
        in_specs=[pl.BlockSpec((1, n, width), lambda bi: (bi, 0, 0)),
                  full((2, width)), full((width, hid)), full((width, hid)), full((hid, LANES))],
        out_specs=pl.BlockSpec((1, n, LANES), lambda bi: (bi, 0, 0)),
        out_shape=jax.ShapeDtypeStruct((bsz, n, LANES), BF16),
        compiler_params=_cparams(("parallel",)),
        name="nsa_compress",
    )(xr, pe2, wt, wb, w2)


_TQ = 128
_TK = 512


def _nsa_kernel(q_ref, kv_ref, gate_ref, kc_ref, vc_ref, mt_ref, e_ref, gx_ref, o_ref,
                m_ref, l_ref, acc_ref, *, s_len):
    tq = _TQ
    rows = NSA_HPG * tq
    n_cmp = s_len // NSA_CMP_STRIDE - 1
    t0 = pl.program_id(1) * tq
    qs = q_ref[0] * jnp.asarray(ATTN_SCALE, BF16)
    lane = lax.broadcasted_iota(jnp.int32, (tq, LANES), 1)

    def stack_q(g):
        keep = jnp.where((lane >= g * HEAD_DIM) & (lane < (g + 1) * HEAD_DIM), 1.0, 0.0).astype(BF16)
        return jnp.concatenate([keep * qs[:, hh * LANES:(hh + 1) * LANES] for hh in range(NSA_HPG)], axis=0)

    def row_pos(width):
        r = lax.broadcasted_iota(jnp.int32, (rows, width), 0)
        return t0 + (r & (tq - 1))

    def col_idx(width):
        return lax.broadcasted_iota(jnp.int32, (rows, width), 1)

    qg = [stack_q(g) for g in range(NSA_GROUPS)]

    kc = kc_ref[0]
    vc = vc_ref[0]
    ncp = kc.shape[0]
    n_idx = col_idx(ncp)
    mask1 = (n_idx * NSA_CMP_STRIDE + (NSA_CMP_LEN - 1) <= row_pos(ncp)) & (n_idx < n_cmp)
    o_cmp, sel = [], []
    jblk = lax.broadcasted_iota(jnp.int32, (LANES, tq), 0)
    posq = t0 + lax.broadcasted_iota(jnp.int32, (LANES, tq), 1)
    blk = posq >> 6
    causal = jblk <= blk
    forced = (jblk == 0) | (jblk == blk) | (jblk == blk - 1)
    n_sel = s_len // NSA_SEL_LEN
    for g in range(NSA_GROUPS):
        s1 = _dot_nt(qg[g], kc)
        s1 = jnp.where(mask1, s1, MASK_VALUE)
        mx = jnp.max(s1, axis=1, keepdims=True)
        p1 = jnp.where(mask1, jnp.exp(s1 - mx), 0.0)
        den = jnp.sum(p1, axis=1, keepdims=True)
        p1 = p1 / jnp.where(den > 0.0, den, 1.0)
        o_cmp.append(_dot(p1.astype(BF16), vc))
        psum = p1[0:tq] + p1[tq:2 * tq] + p1[2 * tq:3 * tq] + p1[3 * tq:4 * tq]
        p_hi = psum.astype(BF16)
        p_lo = (psum - p_hi.astype(F32)).astype(BF16)
        imp_t = _dot_nt(mt_ref[...], p_hi) + _dot_nt(mt_ref[...], p_lo)
        sc = jnp.where(causal, jnp.where(forced, FORCE_SCORE, imp_t), -jnp.inf)
        cnt = jnp.zeros((LANES, tq), F32)
        for k in range(n_sel):
            rk = sc[k:k + 1, :]
            tie = jnp.where(jblk > k, 1.0, 0.0)
            cnt = cnt + jnp.where(rk > sc, 1.0, 0.0) + jnp.where(rk == sc, tie, 0.0)
        sel_t = jnp.where((cnt < NSA_TOP) & causal & (jblk < n_sel), 1.0, 0.0)
        sel.append(sel_t.T.astype(BF16))

    n_chunks = (t0 + tq + _TK - 1) // _TK
    o_slc = []
    for g in range(NSA_GROUPS):
        m_ref[...] = jnp.full((rows, 1), MASK_VALUE, F32)
        l_ref[...] = jnp.zeros((rows, 1), F32)
        acc_ref[...] = jnp.zeros((rows, LANES), F32)
        pos2 = row_pos(_TK)
        kcol = col_idx(_TK)

        def chunk(i, carry, g=g, pos2=pos2, kcol=kcol):
            k0 = pl.multiple_of(i * _TK, _TK)
            ks = kv_ref[0, pl.ds(k0, _TK), 0:LANES]
            vs = kv_ref[0, pl.ds(k0, _TK), LANES:2 * LANES]
            s2 = _dot_nt(qg[g], ks)
            bm = _dot(sel[g], e_ref[:, pl.ds(k0, _TK)])
            bm4 = jnp.concatenate([bm] * NSA_HPG, axis=0)
            valid = (bm4 > 0.5) & (kcol + k0 <= pos2)
            s2 = jnp.where(valid, s2, MASK_VALUE)
            m_old = m_ref[...]
            m_new = jnp.maximum(m_old, jnp.max(s2, axis=1, keepdims=True))
            alpha = jnp.exp(m_old - m_new)
            p = jnp.where(valid, jnp.exp(s2 - m_new), 0.0)
            l_ref[...] = alpha * l_ref[...] + jnp.sum(p, axis=1, keepdims=True)
            acc_ref[...] = alpha * acc_ref[...] + _dot(p.astype(BF16), vs)
            m_ref[...] = m_new
            return carry

        lax.fori_loop(0, n_chunks, chunk, 0)
        o_slc.append(acc_ref[...] / l_ref[...])

    wk = tq + NSA_WINDOW
    w0 = pl.multiple_of(jnp.maximum(t0 - NSA_WINDOW, 0), tq)
    kw = kv_ref[0, pl.ds(w0, wk), 2 * LANES:3 * LANES]
    vw = kv_ref[0, pl.ds(w0, wk), 3 * LANES:4 * LANES]
    pos3 = row_pos(wk)
    kpos = w0 + col_idx(wk)
    mask3 = (kpos <= pos3) & (pos3 - kpos < NSA_WINDOW)
    o_win = []
    for g in range(NSA_GROUPS):
        s3 = jnp.where(mask3, _dot_nt(qg[g], kw), MASK_VALUE)
        mx = jnp.max(s3, axis=1, keepdims=True)
        p3 = jnp.where(mask3, jnp.exp(s3 - mx), 0.0)
        den = jnp.sum(p3, axis=1, keepdims=True)
        o_win.append(_dot(p3.astype(BF16), vw) / den)

    gexp = _split_dot(_sigmoid(gate_ref[0]), gx_ref[...])
    low = lane < HEAD_DIM
    for hh in range(NSA_HPG):
        r0, r1 = hh * tq, (hh + 1) * tq
        y = jnp.zeros((tq, LANES), F32)
        for br, o in enumerate((o_cmp, o_slc, o_win)):
            gate = gexp[:, (br * NSA_HPG + hh) * LANES:(br * NSA_HPG + hh + 1) * LANES]
            y = y + gate * jnp.where(low, o[0][r0:r1], o[1][r0:r1])
        o_ref[0, :, hh * LANES:(hh + 1) * LANES] = y.astype(BF16)


def _nsa_consts(s_len):
    n_cmp_pad = s_len // NSA_CMP_STRIDE
    n_cmp = n_cmp_pad - 1
    n_sel = s_len // NSA_SEL_LEN
    cmp_start = np.arange(n_cmp) * NSA_CMP_STRIDE
    cmp_end = cmp_start + NSA_CMP_LEN - 1
    sel_start = np.arange(n_sel) * NSA_SEL_LEN
    m = ((cmp_start[:, None] <= sel_start[None, :] + NSA_SEL_LEN - 1) & (cmp_end[:, None] >= sel_start[None, :]))
    mt = np.zeros((LANES, n_cmp_pad), np.float32)
    mt[:n_sel, :n_cmp] = m.T
    e = np.zeros((LANES, s_len), np.float32)
    e[np.arange(s_len) // NSA_SEL_LEN, np.arange(s_len)] = 1.0
    gx = np.zeros((LANES, 3 * NSA_HPG * LANES), np.float32)
    for br in range(3):
        for hh in range(NSA_HPG):
            for g in range(NSA_GROUPS):
                head = g * NSA_HPG + hh
                c0 = (br * NSA_HPG + hh) * LANES + g * HEAD_DIM
                gx[head * 3 + br, c0:c0 + HEAD_DIM] = 1.0
    return jnp.asarray(mt, BF16), jnp.asarray(e, BF16), jnp.asarray(gx, BF16)


def _nsa_attention(q, kv4, gate, kc, vc, consts):
    bsz, s, _ = q.shape
    assert s % _TK == 0 and s >= _TQ + NSA_WINDOW and s // NSA_SEL_LEN <= LANES
    mt, e, gx = consts
    ncp = kc.shape[1]
    rows = NSA_HPG * _TQ
    full = lambda arr: pl.BlockSpec(arr.shape, lambda bi, ci: (0,) * arr.ndim)
    return pl.pallas_call(
        functools.partial(_nsa_kernel, s_len=s),
        grid=(bsz, s // _TQ),
        in_specs=[pl.BlockSpec((1, _TQ, 4 * LANES), lambda bi, ci: (bi, ci, 0)),
                  pl.BlockSpec((1, s, 4 * LANES), lambda bi, ci: (bi, 0, 0)),
                  pl.BlockSpec((1, _TQ, LANES), lambda bi, ci: (bi, ci, 0)),
                  pl.BlockSpec((1, ncp, LANES), lambda bi, ci: (bi, 0, 0)),
                  pl.BlockSpec((1, ncp, LANES), lambda bi, ci: (bi, 0, 0)),
                  full(mt), full(e), full(gx)],
        out_specs=pl.BlockSpec((1, _TQ, 4 * LANES), lambda bi, ci: (bi, ci, 0)),
        out_shape=jax.ShapeDtypeStruct((bsz, s, 4 * LANES), BF16),
        scratch_shapes=[pltpu.VMEM((rows, 1), F32), pltpu.VMEM((rows, 1), F32), pltpu.VMEM((rows, LANES), F32)],
        compiler_params=_cparams(("parallel", "arbitrary")),
        name="nsa_attention",
    )(q, kv4, gate, kc, vc, mt, e, gx)


_HC = 64
_H_SETS = 13


def _hgrn_consts():
    c = _HC
    t = np.arange(c)[:, None]
    u = np.arange(c)[None, :]
    sets = [u <= t,
            u > t,
            (u >= (t // 16) * 16) & (u <= t),
            (u >= (t // 4) * 4) & (u <= t)]
    for m in (1, 2, 3):
        sets.append((u > t) & (u <= 16 * m - 1))
    for m in (1, 2, 3):
        sets.append((u > t) & (u <= (t // 16) * 16 + 4 * m - 1))
    for m in (1, 2, 3):
        sets.append((u > t) & (u <= (t // 4) * 4 + m))
    dmat = np.concatenate([x.astype(np.float32) for x in sets], axis=0)

    r = np.arange(4 * c)[:, None] % c
    col = np.arange(4 * c)[None, :]
    slot, s = col // c, col % c
    m16 = (slot == r // 16 - 1) & (s // 16 < r // 16)
    m4 = (slot == (r % 16) // 4 - 1) & (s // 16 == r // 16) & ((s % 16) // 4 < (r % 16) // 4)
    m1 = (slot == r % 4) & (s // 4 == r // 4) & (s % 4 <= r % 4)
    masks = np.stack([m16, m4, m1]).astype(np.float32)
    hr = np.arange(4 * c)[:, None] // c
    hc = np.arange(4 * c)[None, :] // c
    bd = (hr == hc).astype(np.float32)
    return jnp.asarray(dmat, BF16), jnp.asarray(masks, F32), jnp.asarray(bd, F32)


def _hgrn_kernel(u_ref, lbl_ref, ng_ref, d_ref, mk_ref, bd_ref, o_ref, st_ref, *, layer, rb):
    w = HGRN_W
    c = _HC

    @pl.when(pl.program_id(1) == 0)
    def _():
        st_ref[...] = jnp.zeros((w, w), F32)

    lg = [lbl_ref[r:r + 1, :] for r in range(DEPTH)]
    mx = functools.reduce(jnp.maximum, lg)
    ex = [jnp.exp(x - mx) for x in lg]
    tot = functools.reduce(lambda a, b: a + b, ex)
    lb = jnp.zeros((1, w), F32)
    for r in range(1, layer + 1):
        lb = lb + ex[r] / tot

    bd = bd_ref[...]
    lane_head = lax.broadcasted_iota(jnp.int32, (c, w), 1) // HEAD_DIM
    head_keep = [lane_head == h for h in range(HGRN_HEADS)]
    head_keep_b = [jnp.where(m, 1.0, 0.0).astype(BF16) for m in head_keep]

    def stack_heads(x):
        return jnp.concatenate([head_keep_b[h] * x for h in range(HGRN_HEADS)], axis=0)

    def chunk(ci, carry):
        r0 = pl.multiple_of(ci * c, c)
        q = u_ref[0, pl.ds(r0, c), 0:w]
        fz = u_ref[0, pl.ds(r0, c), w:2 * w]
        v = u_ref[0, pl.ds(r0, c), 2 * w:3 * w]
        gz = u_ref[0, pl.ds(r0, c), 3 * w:4 * w]
        f = lb + (1.0 - lb) * _sigmoid(fz)
        logf = jnp.log(f)
        kk = 1.0 - f
        lhi = logf.astype(BF16)
        llo = (logf - lhi.astype(F32)).astype(BF16)
        ex_all = jnp.exp(_dot(d_ref[...], lhi) + _dot(d_ref[...], llo))

        def eset(i):
            return ex_all[i * c:(i + 1) * c, :]

        v_b = v.astype(BF16)
        kq = [q * eset(2), q * eset(3), q]
        zero_k = jnp.zeros_like(kk)
        kk_sets = [[kk * eset(4 + j) for j in range(3)] + [zero_k],
                   [kk * eset(7 + j) for j in range(3)] + [zero_k],
                   [kk] + [kk * eset(10 + j) for j in range(3)]]
        sw = jnp.zeros((HGRN_HEADS * c, HGRN_HEADS * c), F32)
        for lvl in range(3):
            lhs = stack_heads(kq[lvl].astype(BF16))
            rhs = jnp.concatenate([x.astype(BF16) for x in kk_sets[lvl]], axis=0)
            sw = sw + _dot_nt(lhs, rhs) * mk_ref[lvl]
        v_rep = jnp.concatenate([v_b] * HGRN_HEADS, axis=0)
        o_full = _dot(sw.astype(BF16), v_rep)
        o = jnp.zeros((c, w), F32)
        for h in range(HGRN_HEADS):
            o = o + jnp.where(head_keep[h], o_full[h * c:(h + 1) * c, :], 0.0)
        st = st_ref[...]
        o = o + _dot_nt((q * eset(0)).astype(BF16), st.astype(BF16))
        kst = (kk * eset(1)).astype(BF16)
        upd = lax.dot_general(v_b, kst, _TN, preferred_element_type=F32)
        st_ref[...] = ex_all[c - 1:c, :] * st + bd * upd
        ms = _split_dot(o * o, bd_ref[...].astype(BF16)) * (1.0 / HEAD_DIM)
        y = o * lax.rsqrt(ms + LN_EPS) * ng_ref[...]
        y = y * (gz * _sigmoid(gz))
        o_ref[0, pl.ds(r0, c), :] = y.astype(BF16)
        return carry

    lax.fori_loop(0, rb // c, chunk, 0)


def _hgrn(u, lb_logits, norm_g4, consts, layer, rb=512):
    bsz, s, _ = u.shape
    dmat, masks, bd = consts
    full = lambda arr: pl.BlockSpec(arr.shape, lambda bi, ci: (0,) * arr.ndim)
    return pl.pallas_call(
        functools.partial(_hgrn_kernel, layer=layer, rb=rb),
        grid=(bsz, s // rb),
        in_specs=[pl.BlockSpec((1, rb, 4 * HGRN_W), lambda bi, ci: (bi, ci, 0)),
                  full(lb_logits), full(norm_g4), full(dmat), full(masks), full(bd)],
        out_specs=pl.BlockSpec((1, rb, HGRN_W), lambda bi, ci: (bi, ci, 0)),
        out_shape=jax.ShapeDtypeStruct((bsz, s, HGRN_W), BF16),
        scratch_shapes=[pltpu.VMEM((HGRN_W, HGRN_W), F32)],
        compiler_params=_cparams(("parallel", "arbitrary")),
        name="hgrn2",
    )(u, lb_logits, norm_g4, dmat, masks, bd)


def _outproj_kernel(h_ref, yc_ref, yn_ref, yh_ref, wc_ref, wn_ref, wh_ref, g_ref, b_ref, o_ref):
    m = _dot(yc_ref[...], wc_ref[...]) + _dot(yn_ref[...], wn_ref[...]) + _dot(yh_ref[...], wh_ref[...])
    o_ref[...] = _layer_norm(DN_ALPHA * h_ref[...] + m, g_ref[...], b_ref[...])


def _outproj(h, yc, yn, yh, wc, wn, wh, g, b, tm=512):
    t = h.shape[0]
    row = lambda wd: pl.BlockSpec((tm, wd), lambda i: (i, 0))
    full = lambda arr: pl.BlockSpec(arr.shape, lambda i: (0,) * arr.ndim)
    return pl.pallas_call(
        _outproj_kernel,
        grid=(t // tm,),
        in_specs=[row(D_MODEL), row(CONV_CH), row(4 * LANES), row(HGRN_W),
                  full(wc), full(wn), full(wh), full(g), full(b)],
        out_specs=row(D_MODEL),
        out_shape=jax.ShapeDtypeStruct((t, D_MODEL), F32),
        compiler_params=_cparams(("parallel",)),
        name="outproj_ln1",
    )(h, yc, yn, yh, wc, wn, wh, g, b)


_FF_HALO = 8


def _ffn_kernel(h_ref, halo_ref, p_ref, wup_ref, cw_ref, cb_ref, wdn_ref, wpg_ref, wpp_ref, g_ref, b_ref,
                o_ref, acc_ref, *, tm, tf, s_len):
    i = pl.program_id(0)
    j = pl.program_id(1)
    x = h_ref[...].astype(BF16)

    @pl.when(j == 0)
    def _():
        gate = _sigmoid(_dot(x, wpg_ref[...]))
        acc_ref[...] = gate * _dot(p_ref[...].astype(BF16), wpp_ref[...])

    seq_start = (i * tm) % s_len == 0
    halo = jnp.where(seq_start, 0.0, halo_ref[...]).astype(BF16)
    up = _dot(jnp.concatenate([halo, x], axis=0), wup_ref[0])
    cw = cw_ref[0]
    u = (cw[0:1, :] * up[_FF_HALO - 2:_FF_HALO - 2 + tm] + cw[1:2, :] * up[_FF_HALO - 1:_FF_HALO - 1 + tm]
         + cw[2:3, :] * up[_FF_HALO:_FF_HALO + tm] + cb_ref[0])
    val, gate = u[:, :tf], u[:, tf:]
    act = (gate * _sigmoid(gate) * val).astype(BF16)
    acc_ref[...] += _dot(act, wdn_ref[0])

    @pl.when(j == pl.num_programs(1) - 1)
    def _():
        o_ref[...] = _layer_norm(DN_ALPHA * h_ref[...] + acc_ref[...], g_ref[...], b_ref[...])


def _ffn(h, p, wup, cw, cb, wdn, wpg, wpp, g, b, s_len, tm=512):
    t = h.shape[0]
    nf, _, tf2 = wup.shape
    tf = tf2 // 2
    hb = tm // _FF_HALO
    full = lambda arr: pl.BlockSpec(arr.shape, lambda i, j: (0,) * arr.ndim)
    return pl.pallas_call(
        functools.partial(_ffn_kernel, tm=tm, tf=tf, s_len=s_len),
        grid=(t // tm, nf),
        in_specs=[pl.BlockSpec((tm, D_MODEL), lambda i, j: (i, 0)),
                  pl.BlockSpec((_FF_HALO, D_MODEL), lambda i, j: (jnp.maximum(i * hb - 1, 0), 0)),
                  pl.BlockSpec((tm, D_PLE), lambda i, j: (i, 0)),
                  pl.BlockSpec((1, D_MODEL, tf2), lambda i, j: (j, 0, 0)),
                  pl.BlockSpec((1, 8, tf2), lambda i, j: (j, 0, 0)),
                  pl.BlockSpec((1, 1, tf2), lambda i, j: (j, 0, 0)),
                  pl.BlockSpec((1, tf, D_MODEL), lambda i, j: (j, 0, 0)),
                  full(wpg), full(wpp), full(g), full(b)],
        out_specs=pl.BlockSpec((tm, D_MODEL), lambda i, j: (i, 0)),
        out_shape=jax.ShapeDtypeStruct((t, D_MODEL), F32),
        scratch_shapes=[pltpu.VMEM((tm, D_MODEL), F32)],
        compiler_params=_cparams(("parallel", "arbitrary")),
        name="ffn_ple_ln2",
    )(h, h, p, wup, cw, cb, wdn, wpg, wpp, g, b)


def _pack_in_weights(w_in):
    conv = w_in[:, :, 0:512]
    qw = w_in[:, :, 512:1024].reshape(DEPTH, D_MODEL, NSA_GROUPS, NSA_HPG, HEAD_DIM)
    qw = qw.transpose(0, 1, 3, 2, 4).reshape(DEPTH, D_MODEL, 512)
    kv = w_in[:, :, 1024:1792]
    gw = jnp.pad(w_in[:, :, 1792:1816], ((0, 0), (0, 0), (0, LANES - 24)))
    hw = w_in[:, :, 1816:2840]
    return jnp.concatenate([conv, qw, kv, gw, hw], axis=-1).astype(BF16)


def _pack_cmp_weights(pe, w1, w2):
    d = pe.shape[0]
    eye = jnp.eye(NSA_GROUPS, dtype=F32)
    pe2 = jnp.broadcast_to(pe.reshape(d, 2, 16, 1, HEAD_DIM), (d, 2, 16, NSA_GROUPS, HEAD_DIM)).reshape(d, 2, 2048)
    w1r = w1.reshape(d, 2, 16, HEAD_DIM, NSA_CMP_HIDDEN)
    w1e = jnp.einsum('dplkh,ge->dplgkeh', w1r, eye).reshape(d, 2, 2048, NSA_GROUPS * NSA_CMP_HIDDEN)
    w2e = jnp.einsum('dhk,ge->dghek', w2, eye).reshape(d, NSA_GROUPS * NSA_CMP_HIDDEN, LANES)
    return pe2, w1e[:, 0].astype(BF16), w1e[:, 1].astype(BF16), w2e.astype(BF16)


def _pack_out_weights(w_out):
    wc = w_out[:, 0:CONV_CH]
    wn = w_out[:, CONV_CH:CONV_CH + 512].reshape(DEPTH, NSA_GROUPS, NSA_HPG, HEAD_DIM, D_MODEL)
    wn = wn.transpose(0, 2, 1, 3, 4).reshape(DEPTH, 512, D_MODEL)
    wh = w_out[:, CONV_CH + 512:]
    return wc.astype(BF16), wn.astype(BF16), wh.astype(BF16)


def _pack_ffn_weights(w_up, conv_w, conv_b, w_down, nf):
    tf = D_FF // nf

    def tiles(a):
        lead = a.shape[:-1]
        a = a.reshape(lead + (2, nf, tf))
        a = jnp.moveaxis(a, -2, 0)
        return a.reshape((nf,) + lead + (2 * tf,))

    wup = jnp.moveaxis(tiles(w_up), 0, 1).astype(BF16)
    cw = jnp.moveaxis(tiles(jnp.pad(conv_w, ((0, 0), (0, 5), (0, 0)))), 0, 1)
    cb = jnp.moveaxis(tiles(conv_b[:, None, :]), 0, 1)
    wdn = w_down.reshape(DEPTH, nf, tf, D_MODEL).astype(BF16)
    return wup, cw, cb, wdn


def kernel(x, p, w_in, conv_w, conv_b, conv_ln_g, conv_ln_b, cmp_pe_k, cmp_pe_v, cmp_w1_k, cmp_w2_k, cmp_w1_v,
           cmp_w2_v, lb_logits, hgrn_norm_g, w_out, ln1_g, ln1_b, w_up, ffn_conv_w, ffn_conv_b, w_down,
           w_ple_gate, w_ple_proj, ln2_g, ln2_b):
    bsz, s, _ = x.shape
    t = bsz * s
    nf = 2
    w_in_p = _pack_in_weights(w_in)
    pe_k, w1k_t, w1k_b, w2k = _pack_cmp_weights(cmp_pe_k, cmp_w1_k, cmp_w2_k)
    pe_v, w1v_t, w1v_b, w2v = _pack_cmp_weights(cmp_pe_v, cmp_w1_v, cmp_w2_v)
    wc, wn, wh = _pack_out_weights(w_out)
    wup, fcw, fcb, wdn = _pack_ffn_weights(w_up, ffn_conv_w, ffn_conv_b, w_down, nf)
    conv_w_p = jnp.pad(conv_w, ((0, 0), (0, 1), (0, 0)))
    norm_g4 = jnp.tile(hgrn_norm_g, (1, HGRN_HEADS))[:, None, :]
    wpg = w_ple_gate.astype(BF16)
    wpp = w_ple_proj.astype(BF16)
    nsa_consts = _nsa_consts(s)
    hgrn_consts = _hgrn_consts()
    n_rows = s // NSA_CMP_STRIDE

    h = x.reshape(t, D_MODEL)
    for i in range(DEPTH):
        conv_u, q, kcu, vcu, kv4, gate, hg = _inproj(h, w_in_p[i])
        y_conv = _convmod(conv_u.reshape(bsz, s, 2 * CONV_CH), conv_w_p[i], conv_b[i][None], conv_ln_g[i][None],
                          conv_ln_b[i][None])
        kc = _compress(kcu.reshape(bsz, n_rows, NSA_CMP_STRIDE * LANES), pe_k[i], w1k_t[i], w1k_b[i], w2k[i])
        vc = _compress(vcu.reshape(bsz, n_rows, NSA_CMP_STRIDE * LANES), pe_v[i], w1v_t[i], w1v_b[i], w2v[i])
        y_nsa = _nsa_attention(q.reshape(bsz, s, 512), kv4.reshape(bsz, s, 512), gate.reshape(bsz, s, LANES),
                               kc, vc, nsa_consts)
        y_hgrn = _hgrn(hg.reshape(bsz, s, 4 * HGRN_W), lb_logits, norm_g4[i], hgrn_consts, i)
        h1 = _outproj(h, y_conv.reshape(t, CONV_CH), y_nsa.reshape(t, 512), y_hgrn.reshape(t, HGRN_W),
                      wc[i], wn[i], wh[i], ln1_g[i][None], ln1_b[i][None])
        h = _ffn(h1, p[i].reshape(t, D_PLE), wup[i], fcw[i], fcb[i], wdn[i], wpg[i], wpp[i],
                 ln2_g[i][None], ln2_b[i][None], s)
    return h.reshape(bsz, s, D_MODEL)
```

```python
import functools

import numpy as np
import jax
import jax.numpy as jnp
from jax import lax
from jax.experimental import pallas as pl
from jax.experimental.pallas import tpu as pltpu

F32 = jnp.float32
BF16 = jnp.bfloat16

D_MODEL = 1024
DEPTH = 4
D_PLE = 256
HEAD_DIM = 64
CONV_CH = 256
CONV_K = 31
NSA_HEADS = 8
NSA_GROUPS = 2
NSA_HPG = NSA_HEADS // NSA_GROUPS
NSA_CMP_STRIDE = 16
NSA_CMP_LEN = 32
NSA_CMP_HIDDEN = 128
NSA_SEL_LEN = 64
NSA_TOP = 16
NSA_WINDOW = 512
HGRN_HEADS = 4
HGRN_W = HGRN_HEADS * HEAD_DIM
D_FF = 2816
DN_ALPHA = (2 * DEPTH) ** 0.25
ATTN_SCALE = HEAD_DIM ** -0.5
LN_EPS = 1e-5
MASK_VALUE = -1e30
FORCE_SCORE = 1e9

LANES = 128
SUBLANES = 8
VMEM_LIMIT = 56 * 1024 * 1024

_O_CONV, _O_KC, _O_VC, _O_KK, _O_HG, _O_END = 0, 512, 640, 768, 1024, 2048
_R_Q, _R_V, _R_GATE, _R_END = 0, 512, 768, 896

_NT = (((1,), (1,)), ((), ()))
_TN = (((0,), (0,)), ((), ()))


def _cparams(sem):
    return pltpu.CompilerParams(dimension_semantics=sem, vmem_limit_bytes=VMEM_LIMIT)


def _dot(a, b):
    return jnp.dot(a, b, preferred_element_type=F32)


def _dot_nt(a, b):
    return lax.dot_general(a, b, _NT, preferred_element_type=F32)


def _split_dot(x, w):
    hi = x.astype(BF16)
    lo = (x - hi.astype(F32)).astype(BF16)
    return _dot(hi, w) + _dot(lo, w)


def _sigmoid(x):
    return 1.0 / (1.0 + jnp.exp(-x))


def _layer_norm(z, g, b):
    mu = jnp.mean(z, axis=-1, keepdims=True)
    zc = z - mu
    var = jnp.mean(zc * zc, axis=-1, keepdims=True)
    return zc * lax.rsqrt(var + LN_EPS) * g + b


def _inproj_kernel(h_ref, w_ref, wt_ref, conv_ref, kc_ref, vc_ref, kk_ref, hg_ref, qt_ref, vt_ref, gt_ref):
    x = h_ref[...].astype(BF16)
    conv_ref[...] = _dot(x, w_ref[:, _O_CONV:_O_KC])
    kc_ref[...] = _dot(x, w_ref[:, _O_KC:_O_VC]).astype(BF16)
    vc_ref[...] = _dot(x, w_ref[:, _O_VC:_O_KK]).astype(BF16)
    kk_ref[...] = _dot(x, w_ref[:, _O_KK:_O_HG]).astype(BF16)
    hg_ref[...] = _dot(x, w_ref[:, _O_HG:_O_END])
    qt_ref[...] = _dot_nt(wt_ref[_R_Q:_R_V, :], x).astype(BF16)
    vt_ref[...] = _dot_nt(wt_ref[_R_V:_R_GATE, :], x).astype(BF16)
    gt_ref[...] = _dot_nt(wt_ref[_R_GATE:_R_END, :], x)


def _inproj(h, w, wt, tm=512):
    t = h.shape[0]
    widths = (512, 128, 128, 256, 1024)
    dtypes = (F32, BF16, BF16, BF16, F32)
    heights = (512, 256, 128)
    hdtypes = (BF16, BF16, F32)
    return pl.pallas_call(
        _inproj_kernel,
        grid=(t // tm,),
        in_specs=[pl.BlockSpec((tm, D_MODEL), lambda i: (i, 0)),
                  pl.BlockSpec((D_MODEL, _O_END), lambda i: (0, 0)),
                  pl.BlockSpec((_R_END, D_MODEL), lambda i: (0, 0))],
        out_specs=([pl.BlockSpec((tm, wd), lambda i: (i, 0)) for wd in widths]
                   + [pl.BlockSpec((ht, tm), lambda i: (0, i)) for ht in heights]),
        out_shape=([jax.ShapeDtypeStruct((t, wd), dt) for wd, dt in zip(widths, dtypes)]
                   + [jax.ShapeDtypeStruct((ht, t), dt) for ht, dt in zip(heights, hdtypes)]),
        compiler_params=_cparams(("parallel",)),
        name="inproj",
    )(h, w, wt)


_CONV_HALO = 32
_CONV_SUB = 64


def _convmod_kernel(u_ref, w_ref, b_ref, g_ref, beta_ref, o_ref, buf_ref, *, tt):
    @pl.when(pl.program_id(1) == 0)
    def _():
        buf_ref[0:_CONV_HALO, :] = jnp.zeros((_CONV_HALO, CONV_CH), F32)

    u = u_ref[0]
    buf_ref[_CONV_HALO:_CONV_HALO + tt, :] = u[:, :CONV_CH] * _sigmoid(u[:, CONV_CH:])
    w = w_ref[...]
    base = _CONV_HALO - (CONV_K - 1)
    for r in range(tt // _CONV_SUB):
        acc = jnp.broadcast_to(b_ref[...], (_CONV_SUB, CONV_CH))
        for k in range(CONV_K):
            s0 = r * _CONV_SUB + base + k
            acc = acc + w[k:k + 1, :] * buf_ref[s0:s0 + _CONV_SUB, :]
        y = _layer_norm(acc, g_ref[...], beta_ref[...])
        o_ref[0, r * _CONV_SUB:(r + 1) * _CONV_SUB, :] = (y * _sigmoid(y)).astype(BF16)
    buf_ref[0:_CONV_HALO, :] = buf_ref[tt:tt + _CONV_HALO, :]


def _convmod(u, w, b, g, beta, tt=512):
    bsz, s, _ = u.shape
    vec = pl.BlockSpec((1, CONV_CH), lambda bi, ti: (0, 0))
    return pl.pallas_call(
        functools.partial(_convmod_kernel, tt=tt),
        grid=(bsz, s // tt),
        in_specs=[pl.BlockSpec((1, tt, 2 * CONV_CH), lambda bi, ti: (bi, ti, 0)),
                  pl.BlockSpec((CONV_K + 1, CONV_CH), lambda bi, ti: (0, 0)), vec, vec, vec],
        out_specs=pl.BlockSpec((1, tt, CONV_CH), lambda bi, ti: (bi, ti, 0)),
        out_shape=jax.ShapeDtypeStruct((bsz, s, CONV_CH), BF16),
        scratch_shapes=[pltpu.VMEM((tt + _CONV_HALO, CONV_CH), F32)],
        compiler_params=_cparams(("parallel", "arbitrary")),
        name="convmod",
    )(u, w, b, g, beta)


def _gelu_tanh(x):
    return 0.5 * x * (1.0 + jnp.tanh(0.7978845608028654 * (x + 0.044715 * (x * x * x))))


def _compress_kernel(x_ref, pe_ref, wt_ref, wb_ref, w2_ref, o_ref, *, feature_major):
    x = x_ref[0].astype(F32)
    n = x.shape[0]
    top = _dot((x + pe_ref[0:1, :]).astype(BF16), wt_ref[...])
    bot = _dot((x + pe_ref[1:2, :]).astype(BF16), wb_ref[...])
    hid = top + pltpu.roll(bot, n - 1, 0)
    act = _gelu_tanh(hid).astype(BF16)
    if feature_major:
        o_ref[0] = _dot_nt(w2_ref[...], act).astype(BF16)
    else:
        o_ref[0] = _dot(act, w2_ref[...]).astype(BF16)


def _compress(xr, pe2, wt, wb, w2, feature_major):
    bsz, n, width = xr.shape
    hid = wt.shape[1]
    full = lambda shape: pl.BlockSpec(shape, lambda bi: (0,) * len(shape))
    oshape = (LANES, n) if feature_major else (n, LANES)
    return pl.pallas_call(
        functools.partial(_compress_kernel, feature_major=feature_major),
        grid=(bsz,),
        in_specs=[pl.BlockSpec((1, n, width), lambda bi: (bi, 0, 0)),
                  full((2, width)), full((width, hid)), full((width, hid)), full(w2.shape)],
        out_specs=pl.BlockSpec((1,) + oshape, lambda bi: (bi, 0, 0)),
        out_shape=jax.ShapeDtypeStruct((bsz,) + oshape, BF16),
        compiler_params=_cparams(("parallel",)),
        name="nsa_compress",
    )(xr, pe2, wt, wb, w2)


_TQ = 128
_TK = 512


def _nsa_kernel(qt_ref, kk_ref, vt_ref, gt_ref, kc_ref, vct_ref, mt_ref, o_ref,
                sc_ref, cnt_ref, selb_ref, m_ref, acc_ref, *, s_len):
    tq = _TQ
    cols = NSA_HPG * tq
    n_cmp = s_len // NSA_CMP_STRIDE - 1
    n_sel = s_len // NSA_SEL_LEN
    n_grp = n_sel // SUBLANES
    t0 = pl.program_id(1) * tq
    qt = qt_ref[...] * jnp.asarray(ATTN_SCALE, BF16)
    row_grp = lax.broadcasted_iota(jnp.int32, (LANES, tq), 0) // HEAD_DIM
    keep = [jnp.where(row_grp == g, 1.0, 0.0).astype(BF16) for g in range(NSA_GROUPS)]

    def stack_q(g):
        return jnp.concatenate([keep[g] * qt[hh * LANES:(hh + 1) * LANES, :] for hh in range(NSA_HPG)], axis=1)

    qg = [stack_q(g) for g in range(NSA_GROUPS)]

    def per_head(s, fn):
        return jnp.concatenate([fn(s[:, hh * tq:(hh + 1) * tq]) for hh in range(NSA_HPG)], axis=1)

    def own_rows(v_t, g):
        kb = jnp.where(lax.broadcasted_iota(jnp.int32, v_t.shape, 0) // HEAD_DIM == g, 1.0, 0.0).astype(BF16)
        return kb * v_t + (1.0 - kb)

    def normalise(acc, g):
        oth = (1 - g) * HEAD_DIM
        return acc * (1.0 / acc[oth:oth + 1, :])

    kc = kc_ref[0]
    vct = vct_ref[0]
    ncp = kc.shape[0]
    n_i = lax.broadcasted_iota(jnp.int32, (ncp, tq), 0)
    pos1 = t0 + lax.broadcasted_iota(jnp.int32, (ncp, tq), 1)
    ok1 = (n_i * NSA_CMP_STRIDE + (NSA_CMP_LEN - 1) <= pos1) & (n_i < n_cmp)
    bias1 = jnp.where(ok1, 0.0, MASK_VALUE)
    live1 = jnp.where(ok1, 1.0, 0.0)
    jblk = lax.broadcasted_iota(jnp.int32, (n_sel, tq), 0)
    blk = (t0 + lax.broadcasted_iota(jnp.int32, (n_sel, tq), 1)) >> 6
    causal = jblk <= blk
    forced = (jblk == 0) | (jblk == blk) | (jblk == blk - 1)
    n_live_blocks = (t0 + tq) // NSA_SEL_LEN
    jrow = lax.broadcasted_iota(jnp.int32, (SUBLANES, tq), 0)
    o_cmp = []
    for g in range(NSA_GROUPS):
        s1 = per_head(_dot(kc, qg[g]), lambda x: x + bias1)
        mx = jnp.max(s1, axis=0, keepdims=True)
        p1 = per_head(jnp.exp(s1 - mx), lambda x: x * live1)
        den = jnp.sum(p1, axis=0, keepdims=True)
        p1 = p1 * (1.0 / jnp.where(den > 0.0, den, 1.0))
        o_cmp.append(_dot(vct, p1.astype(BF16)))
        psum = p1[:, 0:tq] + p1[:, tq:2 * tq] + p1[:, 2 * tq:3 * tq] + p1[:, 3 * tq:4 * tq]
        p_hi = psum.astype(BF16)
        p_lo = (psum - p_hi.astype(F32)).astype(BF16)
        imp = _dot(mt_ref[...], p_hi) + _dot(mt_ref[...], p_lo)
        sc_ref[...] = jnp.where(causal, jnp.where(forced, FORCE_SCORE, imp), -jnp.inf)
        cnt_ref[...] = jnp.zeros((n_sel, tq), F32)
        for kb in range(n_grp):
            @pl.when(kb * SUBLANES < n_live_blocks)
            def _(kb=kb):
                sc = sc_ref[...]
                grp = [sc[j * SUBLANES:(j + 1) * SUBLANES, :] for j in range(n_grp)]
                cnt = [cnt_ref[j * SUBLANES:(j + 1) * SUBLANES, :] for j in range(n_grp)]
                for k in range(kb * SUBLANES, (kb + 1) * SUBLANES):
                    rk = sc[k:k + 1, :]
                    for j in range(n_grp):
                        if j > kb:
                            inc = jnp.where(rk >= grp[j], 1.0, 0.0)
                        elif j < kb:
                            inc = jnp.where(rk > grp[j], 1.0, 0.0)
                        else:
                            tie = jnp.where(jrow + j * SUBLANES > k, 1.0, 0.0)
                            inc = jnp.where(rk > grp[j], 1.0, jnp.where(rk == grp[j], tie, 0.0))
                        cnt[j] = cnt[j] + inc
                for j in range(n_grp):
                    cnt_ref[j * SUBLANES:(j + 1) * SUBLANES, :] = cnt[j]
        selb_ref[g] = jnp.where((cnt_ref[...] < NSA_TOP) & causal, 0.0, MASK_VALUE)

    n_chunks = (t0 + tq + _TK - 1) // _TK
    blocks_per_chunk = _TK // NSA_SEL_LEN
    krow = lax.broadcasted_iota(jnp.int32, (_TK, tq), 0)
    posq = t0 + lax.broadcasted_iota(jnp.int32, (_TK, tq), 1)
    o_slc = []
    for g in range(NSA_GROUPS):
        m_ref[...] = jnp.full((1, cols), MASK_VALUE, F32)
        acc_ref[...] = jnp.zeros((LANES, cols), F32)

        def chunk(i, carry, g=g):
            k0 = pl.multiple_of(i * _TK, _TK)
            ks = kk_ref[0, pl.ds(k0, _TK), 0:LANES]
            vs_t = own_rows(vt_ref[0:LANES, pl.ds(k0, _TK)], g)
            rows = [jnp.broadcast_to(selb_ref[g, pl.ds(i * blocks_per_chunk + j, 1), :], (NSA_SEL_LEN, tq))
                    for j in range(blocks_per_chunk)]
            bias = jnp.where(krow + k0 <= posq, jnp.concatenate(rows, axis=0), MASK_VALUE)
            s2 = per_head(_dot(ks, qg[g]), lambda x: x + bias)
            m_old = m_ref[...]
            m_new = jnp.maximum(m_old, jnp.max(s2, axis=0, keepdims=True))
            p = jnp.exp(s2 - m_new).astype(BF16)
            acc_ref[...] = jnp.exp(m_old - m_new) * acc_ref[...] + _dot(vs_t, p)
            m_ref[...] = m_new
            return carry

        lax.fori_loop(0, n_chunks, chunk, 0)
        o_slc.append(normalise(acc_ref[...], g))

    wk = tq + NSA_WINDOW
    w0 = pl.multiple_of(jnp.maximum(t0 - NSA_WINDOW, 0), tq)
    kw = kk_ref[0, pl.ds(w0, wk), LANES:2 * LANES]
    vw_t = vt_ref[LANES:2 * LANES, pl.ds(w0, wk)]
    kpos = w0 + lax.broadcasted_iota(jnp.int32, (wk, tq), 0)
    pos3 = t0 + lax.broadcasted_iota(jnp.int32, (wk, tq), 1)
    bias3 = jnp.where((kpos <= pos3) & (pos3 - kpos < NSA_WINDOW), 0.0, MASK_VALUE)
    o_win = []
    for g in range(NSA_GROUPS):
        s3 = per_head(_dot(kw, qg[g]), lambda x: x + bias3)
        p3 = jnp.exp(s3 - jnp.max(s3, axis=0, keepdims=True)).astype(BF16)
        o_win.append(normalise(_dot(own_rows(vw_t, g), p3), g))

    gates = _sigmoid(gt_ref[...])
    first = row_grp == 0
    for hh in range(NSA_HPG):
        y = jnp.zeros((LANES, tq), F32)
        for br, o in enumerate((o_cmp, o_slc, o_win)):
            r_a, r_b = hh * 3 + br, (NSA_HPG + hh) * 3 + br
            y = y + jnp.where(first, gates[r_a:r_a + 1, :] * o[0][:, hh * tq:(hh + 1) * tq],
                              gates[r_b:r_b + 1, :] * o[1][:, hh * tq:(hh + 1) * tq])
        o_ref[0, :, hh * LANES:(hh + 1) * LANES] = y.T.astype(BF16)


def _nsa_consts(s_len):
    n_cmp_pad = s_len // NSA_CMP_STRIDE
    n_cmp = n_cmp_pad - 1
    n_sel = s_len // NSA_SEL_LEN
    cmp_start = np.arange(n_cmp) * NSA_CMP_STRIDE
    cmp_end = cmp_start + NSA_CMP_LEN - 1
    sel_start = np.arange(n_sel) * NSA_SEL_LEN
    m = ((cmp_start[:, None] <= sel_start[None, :] + NSA_SEL_LEN - 1) & (cmp_end[:, None] >= sel_start[None, :]))
    mt = np.zeros((n_sel, n_cmp_pad), np.float32)
    mt[:, :n_cmp] = m.T
    return jnp.asarray(mt, BF16)


def _nsa_attention(qt, kk, vt, gt, kc, vct, mt, bsz, s):
    assert s % _TK == 0 and s >= _TQ + NSA_WINDOW and (s // NSA_SEL_LEN) % SUBLANES == 0
    ncp = kc.shape[1]
    n_sel = s // NSA_SEL_LEN
    cols = NSA_HPG * _TQ
    nq = s // _TQ
    return pl.pallas_call(
        functools.partial(_nsa_kernel, s_len=s),
        grid=(bsz, nq),
        in_specs=[pl.BlockSpec((4 * LANES, _TQ), lambda bi, ci: (0, bi * nq + ci)),
                  pl.BlockSpec((1, s, 2 * LANES), lambda bi, ci: (bi, 0, 0)),
                  pl.BlockSpec((2 * LANES, s), lambda bi, ci: (0, bi)),
                  pl.BlockSpec((LANES, _TQ), lambda bi, ci: (0, bi * nq + ci)),
                  pl.BlockSpec((1, ncp, LANES), lambda bi, ci: (bi, 0, 0)),
                  pl.BlockSpec((1, LANES, ncp), lambda bi, ci: (bi, 0, 0)),
                  pl.BlockSpec(mt.shape, lambda bi, ci: (0, 0))],
        out_specs=pl.BlockSpec((1, _TQ, 4 * LANES), lambda bi, ci: (bi, ci, 0)),
        out_shape=jax.ShapeDtypeStruct((bsz, s, 4 * LANES), BF16),
        scratch_shapes=[pltpu.VMEM((n_sel, _TQ), F32), pltpu.VMEM((n_sel, _TQ), F32),
                        pltpu.VMEM((NSA_GROUPS, n_sel, _TQ), F32),
                        pltpu.VMEM((1, cols), F32), pltpu.VMEM((LANES, cols), F32)],
        compiler_params=_cparams(("parallel", "arbitrary")),
        name="nsa_attention",
    )(qt, kk, vt, gt, kc, vct, mt)


_HC = 64


def _hgrn_consts():
    c = _HC
    t = np.arange(c)[:, None]
    u = np.arange(c)[None, :]
    sets = [u <= t,
            u > t,
            (u >= (t // 16) * 16) & (u <= t),
            (u >= (t // 4) * 4) & (u <= t)]
    for m in (1, 2, 3):
        sets.append((u > t) & (u <= 16 * m - 1))
    for m in (1, 2, 3):
        sets.append((u > t) & (u <= (t // 16) * 16 + 4 * m - 1))
    for m in (1, 2, 3):
        sets.append((u > t) & (u <= (t // 4) * 4 + m))
    dmat = np.concatenate([x.astype(np.float32) for x in sets], axis=0)

    r = np.arange(4 * c)[:, None] % c
    col = np.arange(4 * c)[None, :]
    slot, s = col // c, col % c
    m16 = (slot == r // 16 - 1) & (s // 16 < r // 16)
    m4 = (slot == (r % 16) // 4 - 1) & (s // 16 == r // 16) & ((s % 16) // 4 < (r % 16) // 4)
    m1 = (slot == r % 4) & (s // 4 == r // 4) & (s % 4 <= r % 4)
    masks = np.stack([m16, m4, m1]).astype(np.float32)
    hr = np.arange(4 * c)[:, None] // c
    hc = np.arange(4 * c)[None, :] // c
    bd = (hr == hc).astype(np.float32)
    return jnp.asarray(dmat, BF16), jnp.asarray(masks, F32), jnp.asarray(bd, F32)


def _hgrn_kernel(u_ref, lbl_ref, ng_ref, d_ref, mk_ref, bd_ref, o_ref, st_ref, *, layer, rb):
    w = HGRN_W
    c = _HC

    @pl.when(pl.program_id(1) == 0)
    def _():
        st_ref[...] = jnp.zeros((w, w), F32)

    lg = [lbl_ref[r:r + 1, :] for r in range(DEPTH)]
    mx = functools.reduce(jnp.maximum, lg)
    ex = [jnp.exp(x - mx) for x in lg]
    tot = functools.reduce(lambda a, b: a + b, ex)
    lb = jnp.zeros((1, w), F32)
    for r in range(1, layer + 1):
        lb = lb + ex[r] / tot

    bd = bd_ref[...]
    lane_head = lax.broadcasted_iota(jnp.int32, (c, w), 1) // HEAD_DIM
    head_keep = [lane_head == h for h in range(HGRN_HEADS)]
    head_keep_b = [jnp.where(m, 1.0, 0.0).astype(BF16) for m in head_keep]

    def stack_heads(x):
        return jnp.concatenate([head_keep_b[h] * x for h in range(HGRN_HEADS)], axis=0)

    def chunk(ci, carry):
        r0 = pl.multiple_of(ci * c, c)
        q = u_ref[0, pl.ds(r0, c), 0:w]
        fz = u_ref[0, pl.ds(r0, c), w:2 * w]
        v = u_ref[0, pl.ds(r0, c), 2 * w:3 * w]
        gz = u_ref[0, pl.ds(r0, c), 3 * w:4 * w]
        f = lb + (1.0 - lb) * _sigmoid(fz)
        logf = jnp.log(f)
        kk = 1.0 - f
        lhi = logf.astype(BF16)
        llo = (logf - lhi.astype(F32)).astype(BF16)
        ex_all = jnp.exp(_dot(d_ref[...], lhi) + _dot(d_ref[...], llo))

        def eset(i):
            return ex_all[i * c:(i + 1) * c, :]

        v_b = v.astype(BF16)
        kq = [q * eset(2), q * eset(3), q]
        zero_k = jnp.zeros_like(kk)
        kk_sets = [[kk * eset(4 + j) for j in range(3)] + [zero_k],
                   [kk * eset(7 + j) for j in range(3)] + [zero_k],
                   [kk] + [kk * eset(10 + j) for j in range(3)]]
        sw = jnp.zeros((HGRN_HEADS * c, HGRN_HEADS * c), F32)
        for lvl in range(3):
            lhs = stack_heads(kq[lvl].astype(BF16))
            rhs = jnp.concatenate([x.astype(BF16) for x in kk_sets[lvl]], axis=0)
            sw = sw + _dot_nt(lhs, rhs) * mk_ref[lvl]
        v_rep = jnp.concatenate([v_b] * HGRN_HEADS, axis=0)
        o_full = _dot(sw.astype(BF16), v_rep)
        o = jnp.zeros((c, w), F32)
        for h in range(HGRN_HEADS):
            o = o + jnp.where(head_keep[h], o_full[h * c:(h + 1) * c, :], 0.0)
        st = st_ref[...]
        o = o + _dot_nt((q * eset(0)).astype(BF16), st.astype(BF16))
        kst = (kk * eset(1)).astype(BF16)
        upd = lax.dot_general(v_b, kst, _TN, preferred_element_type=F32)
        st_ref[...] = ex_all[c - 1:c, :] * st + bd * upd
        ms = _split_dot(o * o, bd_ref[...].astype(BF16)) * (1.0 / HEAD_DIM)
        y = o * lax.rsqrt(ms + LN_EPS) * ng_ref[...]
        y = y * (gz * _sigmoid(gz))
        o_ref[0, pl.ds(r0, c), :] = y.astype(BF16)
        return carry

    lax.fori_loop(0, rb // c, chunk, 0)


def _hgrn(u, lb_logits, norm_g4, consts, layer, rb=512):
    bsz, s, _ = u.shape
    dmat, masks, bd = consts
    full = lambda arr: pl.BlockSpec(arr.shape, lambda bi, ci: (0,) * arr.ndim)
    return pl.pallas_call(
        functools.partial(_hgrn_kernel, layer=layer, rb=rb),
        grid=(bsz, s // rb),
        in_specs=[pl.BlockSpec((1, rb, 4 * HGRN_W), lambda bi, ci: (bi, ci, 0)),
                  full(lb_logits), full(norm_g4), full(dmat), full(masks), full(bd)],
        out_specs=pl.BlockSpec((1, rb, HGRN_W), lambda bi, ci: (bi, ci, 0)),
        out_shape=jax.ShapeDtypeStruct((bsz, s, HGRN_W), BF16),
        scratch_shapes=[pltpu.VMEM((HGRN_W, HGRN_W), F32)],
        compiler_params=_cparams(("parallel", "arbitrary")),
        name="hgrn2",
    )(u, lb_logits, norm_g4, dmat, masks, bd)


def _outproj_kernel(h_ref, yc_ref, yn_ref, yh_ref, wc_ref, wn_ref, wh_ref, g_ref, b_ref, o_ref):
    m = _dot(yc_ref[...], wc_ref[...]) + _dot(yn_ref[...], wn_ref[...]) + _dot(yh_ref[...], wh_ref[...])
    o_ref[...] = _layer_norm(DN_ALPHA * h_ref[...] + m, g_ref[...], b_ref[...])


def _outproj(h, yc, yn, yh, wc, wn, wh, g, b, tm=512):
    t = h.shape[0]
    row = lambda wd: pl.BlockSpec((tm, wd), lambda i: (i, 0))
    full = lambda arr: pl.BlockSpec(arr.shape, lambda i: (0,) * arr.ndim)
    return pl.pallas_call(
        _outproj_kernel,
        grid=(t // tm,),
        in_specs=[row(D_MODEL), row(CONV_CH), row(4 * LANES), row(HGRN_W),
                  full(wc), full(wn), full(wh), full(g), full(b)],
        out_specs=row(D_MODEL),
        out_shape=jax.ShapeDtypeStruct((t, D_MODEL), F32),
        compiler_params=_cparams(("parallel",)),
        name="outproj_ln1",
    )(h, yc, yn, yh, wc, wn, wh, g, b)


_FF_HALO = 8


def _ffn_kernel(h_ref, halo_ref, p_ref, wup_ref, cw_ref, cb_ref, wdn_ref, wpg_ref, wpp_ref, g_ref, b_ref,
                o_ref, acc_ref, *, tm, tf, s_len):
    i = pl.program_id(0)
    j = pl.program_id(1)
    x = h_ref[...].astype(BF16)

    @pl.when(j == 0)
    def _():
        gate = _sigmoid(_dot(x, wpg_ref[...]))
        acc_ref[...] = gate * _dot(p_ref[...].astype(BF16), wpp_ref[...])

    seq_start = (i * tm) % s_len == 0
    halo = jnp.where(seq_start, 0.0, halo_ref[...]).astype(BF16)
    up = _dot(jnp.concatenate([halo, x], axis=0), wup_ref[0])
    cw = cw_ref[0]
    u = (cw[0:1, :] * up[_FF_HALO - 2:_FF_HALO - 2 + tm] + cw[1:2, :] * up[_FF_HALO - 1:_FF_HALO - 1 + tm]
         + cw[2:3, :] * up[_FF_HALO:_FF_HALO + tm] + cb_ref[0])
    val, gate = u[:, :tf], u[:, tf:]
    act = (gate * _sigmoid(gate) * val).astype(BF16)
    acc_ref[...] += _dot(act, wdn_ref[0])

    @pl.when(j == pl.num_programs(1) - 1)
    def _():
        o_ref[...] = _layer_norm(DN_ALPHA * h_ref[...] + acc_ref[...], g_ref[...], b_ref[...])


def _ffn(h, p, wup, cw, cb, wdn, wpg, wpp, g, b, s_len, tm=512):
    t = h.shape[0]
    nf, _, tf2 = wup.shape
    tf = tf2 // 2
    hb = tm // _FF_HALO
    full = lambda arr: pl.BlockSpec(arr.shape, lambda i, j: (0,) * arr.ndim)
    return pl.pallas_call(
        functools.partial(_ffn_kernel, tm=tm, tf=tf, s_len=s_len),
        grid=(t // tm, nf),
        in_specs=[pl.BlockSpec((tm, D_MODEL), lambda i, j: (i, 0)),
                  pl.BlockSpec((_FF_HALO, D_MODEL), lambda i, j: (jnp.maximum(i * hb - 1, 0), 0)),
                  pl.BlockSpec((tm, D_PLE), lambda i, j: (i, 0)),
                  pl.BlockSpec((1, D_MODEL, tf2), lambda i, j: (j, 0, 0)),
                  pl.BlockSpec((1, 8, tf2), lambda i, j: (j, 0, 0)),
                  pl.BlockSpec((1, 1, tf2), lambda i, j: (j, 0, 0)),
                  pl.BlockSpec((1, tf, D_MODEL), lambda i, j: (j, 0, 0)),
                  full(wpg), full(wpp), full(g), full(b)],
        out_specs=pl.BlockSpec((tm, D_MODEL), lambda i, j: (i, 0)),
        out_shape=jax.ShapeDtypeStruct((t, D_MODEL), F32),
        scratch_shapes=[pltpu.VMEM((tm, D_MODEL), F32)],
        compiler_params=_cparams(("parallel", "arbitrary")),
        name="ffn_ple_ln2",
    )(h, h, p, wup, cw, cb, wdn, wpg, wpp, g, b)


def _pack_in_weights(w_in):
    kv0 = 1024
    col = lambda k: w_in[:, :, kv0 + k * LANES:kv0 + (k + 1) * LANES]
    w_nat = jnp.concatenate([w_in[:, :, 0:512], col(0), col(1), col(2), col(4), w_in[:, :, 1816:2840]], axis=-1)
    qw = w_in[:, :, 512:1024].reshape(DEPTH, D_MODEL, NSA_GROUPS, NSA_HPG, HEAD_DIM)
    qw = qw.transpose(0, 1, 3, 2, 4).reshape(DEPTH, D_MODEL, 512)
    gw = jnp.pad(w_in[:, :, 1792:1816], ((0, 0), (0, 0), (0, LANES - 24)))
    w_t = jnp.concatenate([qw, col(3), col(5), gw], axis=-1).transpose(0, 2, 1)
    return w_nat.astype(BF16), w_t.astype(BF16)


def _pack_cmp_weights(pe, w1, w2):
    d = pe.shape[0]
    eye = jnp.eye(NSA_GROUPS, dtype=F32)
    pe2 = jnp.broadcast_to(pe.reshape(d, 2, 16, 1, HEAD_DIM), (d, 2, 16, NSA_GROUPS, HEAD_DIM)).reshape(d, 2, 2048)
    w1r = w1.reshape(d, 2, 16, HEAD_DIM, NSA_CMP_HIDDEN)
    w1e = jnp.einsum('dplkh,ge->dplgkeh', w1r, eye).reshape(d, 2, 2048, NSA_GROUPS * NSA_CMP_HIDDEN)
    w2e = jnp.einsum('dhk,ge->dghek', w2, eye).reshape(d, NSA_GROUPS * NSA_CMP_HIDDEN, LANES)
    return pe2, w1e[:, 0].astype(BF16), w1e[:, 1].astype(BF16), w2e.astype(BF16), w2e.transpose(0, 2, 1).astype(BF16)


def _pack_out_weights(w_out):
    wc = w_out[:, 0:CONV_CH]
    wn = w_out[:, CONV_CH:CONV_CH + 512].reshape(DEPTH, NSA_GROUPS, NSA_HPG, HEAD_DIM, D_MODEL)
    wn = wn.transpose(0, 2, 1, 3, 4).reshape(DEPTH, 512, D_MODEL)
    wh = w_out[:, CONV_CH + 512:]
    return wc.astype(BF16), wn.astype(BF16), wh.astype(BF16)


def _pack_ffn_weights(w_up, conv_w, conv_b, w_down, nf):
    tf = D_FF // nf

    def tiles(a):
        lead = a.shape[:-1]
        a = a.reshape(lead + (2, nf, tf))
        a = jnp.moveaxis(a, -2, 0)
        return a.reshape((nf,) + lead + (2 * tf,))

    wup = jnp.moveaxis(tiles(w_up), 0, 1).astype(BF16)
    cw = jnp.moveaxis(tiles(jnp.pad(conv_w, ((0, 0), (0, 5), (0, 0)))), 0, 1)
    cb = jnp.moveaxis(tiles(conv_b[:, None, :]), 0, 1)
    wdn = w_down.reshape(DEPTH, nf, tf, D_MODEL).astype(BF16)
    return wup, cw, cb, wdn


def kernel(x, p, w_in, conv_w, conv_b, conv_ln_g, conv_ln_b, cmp_pe_k, cmp_pe_v, cmp_w1_k, cmp_w2_k, cmp_w1_v,
           cmp_w2_v, lb_logits, hgrn_norm_g, w_out, ln1_g, ln1_b, w_up, ffn_conv_w, ffn_conv_b, w_down,
           w_ple_gate, w_ple_proj, ln2_g, ln2_b):
    bsz, s, _ = x.shape
    t = bsz * s
    nf = 2
    w_nat, w_t = _pack_in_weights(w_in)
    pe_k, w1k_t, w1k_b, w2k, _ = _pack_cmp_weights(cmp_pe_k, cmp_w1_k, cmp_w2_k)
    pe_v, w1v_t, w1v_b, _, w2v_t = _pack_cmp_weights(cmp_pe_v, cmp_w1_v, cmp_w2_v)
    wc, wn, wh = _pack_out_weights(w_out)
    wup, fcw, fcb, wdn = _pack_ffn_weights(w_up, ffn_conv_w, ffn_conv_b, w_down, nf)
    conv_w_p = jnp.pad(conv_w, ((0, 0), (0, 1), (0, 0)))
    norm_g4 = jnp.tile(hgrn_norm_g, (1, HGRN_HEADS))[:, None, :]
    wpg = w_ple_gate.astype(BF16)
    wpp = w_ple_proj.astype(BF16)
    mt = _nsa_consts(s)
    hgrn_consts = _hgrn_consts()
    n_rows = s // NSA_CMP_STRIDE

    h = x.reshape(t, D_MODEL)
    for i in range(DEPTH):
        conv_u, kcu, vcu, kk, hg, qt, vt, gt = _inproj(h, w_nat[i], w_t[i])
        y_conv = _convmod(conv_u.reshape(bsz, s, 2 * CONV_CH), conv_w_p[i], conv_b[i][None], conv_ln_g[i][None],
                          conv_ln_b[i][None])
        kc = _compress(kcu.reshape(bsz, n_rows, NSA_CMP_STRIDE * LANES), pe_k[i], w1k_t[i], w1k_b[i], w2k[i], False)
        vct = _compress(vcu.reshape(bsz, n_rows, NSA_CMP_STRIDE * LANES), pe_v[i], w1v_t[i], w1v_b[i], w2v_t[i], True)
        y_nsa = _nsa_attention(qt, kk.reshape(bsz, s, 2 * LANES), vt, gt, kc, vct, mt, bsz, s)
        y_hgrn = _hgrn(hg.reshape(bsz, s, 4 * HGRN_W), lb_logits, norm_g4[i], hgrn_consts, i)
        h1 = _outproj(h, y_conv.reshape(t, CONV_CH), y_nsa.reshape(t, 512), y_hgrn.reshape(t, HGRN_W),
                      wc[i], wn[i], wh[i], ln1_g[i][None], ln1_b[i][None])
        h = _ffn(h1, p[i].reshape(t, D_PLE), wup[i], fcw[i], fcb[i], wdn[i], wpg[i], wpp[i],
                 ln2_g[i][None], ln2_b[i][None], s)
    return h.reshape(bsz, s, D_MODEL)
```

```python
import functools

import numpy as np
import jax
import jax.numpy as jnp
from jax import lax
from jax.experimental import pallas as pl
from jax.experimental.pallas import tpu as pltpu

F32 = jnp.float32
BF16 = jnp.bfloat16

D_MODEL = 1024
DEPTH = 4
D_PLE = 256
HEAD_DIM = 64
CONV_CH = 256
CONV_K = 31
NSA_HEADS = 8
NSA_GROUPS = 2
NSA_HPG = NSA_HEADS // NSA_GROUPS
NSA_CMP_STRIDE = 16
NSA_CMP_LEN = 32
NSA_CMP_HIDDEN = 128
NSA_SEL_LEN = 64
NSA_TOP = 16
NSA_WINDOW = 512
HGRN_HEADS = 4
HGRN_W = HGRN_HEADS * HEAD_DIM
D_FF = 2816
DN_ALPHA = (2 * DEPTH) ** 0.25
ATTN_SCALE = HEAD_DIM ** -0.5
LN_EPS = 1e-5
MASK_VALUE = -1e30
FORCE_SCORE = 1e9

LANES = 128
SUBLANES = 8
VMEM_LIMIT = 56 * 1024 * 1024

_O_CONV, _O_KC, _O_VC, _O_KK, _O_HG, _O_END = 0, 512, 640, 768, 1024, 2048
_R_Q, _R_V, _R_GATE, _R_END = 0, 512, 768, 896

_NT = (((1,), (1,)), ((), ()))
_TN = (((0,), (0,)), ((), ()))


def _cparams(sem):
    return pltpu.CompilerParams(dimension_semantics=sem, vmem_limit_bytes=VMEM_LIMIT)


def _dot(a, b):
    return jnp.dot(a, b, preferred_element_type=F32)


def _dot_nt(a, b):
    return lax.dot_general(a, b, _NT, preferred_element_type=F32)


def _split_dot(x, w):
    hi = x.astype(BF16)
    lo = (x - hi.astype(F32)).astype(BF16)
    return _dot(hi, w) + _dot(lo, w)


def _sigmoid(x):
    return 1.0 / (1.0 + jnp.exp(-x))


def _layer_norm(z, g, b):
    mu = jnp.mean(z, axis=-1, keepdims=True)
    zc = z - mu
    var = jnp.mean(zc * zc, axis=-1, keepdims=True)
    return zc * lax.rsqrt(var + LN_EPS) * g + b


def _inproj_kernel(h_ref, w_ref, wt_ref, conv_ref, kc_ref, vc_ref, kk_ref, hg_ref, qt_ref, vt_ref, gt_ref):
    x = h_ref[...].astype(BF16)
    conv_ref[...] = _dot(x, w_ref[:, _O_CONV:_O_KC])
    kc_ref[...] = _dot(x, w_ref[:, _O_KC:_O_VC]).astype(BF16)
    vc_ref[...] = _dot(x, w_ref[:, _O_VC:_O_KK]).astype(BF16)
    kk_ref[...] = _dot(x, w_ref[:, _O_KK:_O_HG]).astype(BF16)
    hg_ref[...] = _dot(x, w_ref[:, _O_HG:_O_END])
    qt_ref[...] = _dot_nt(wt_ref[_R_Q:_R_V, :], x).astype(BF16)
    vt_ref[...] = _dot_nt(wt_ref[_R_V:_R_GATE, :], x).astype(BF16)
    gt_ref[...] = _dot_nt(wt_ref[_R_GATE:_R_END, :], x)


def _inproj(h, w, wt, tm=512):
    t = h.shape[0]
    widths = (512, 128, 128, 256, 1024)
    dtypes = (F32, BF16, BF16, BF16, F32)
    heights = (512, 256, 128)
    hdtypes = (BF16, BF16, F32)
    return pl.pallas_call(
        _inproj_kernel,
        grid=(t // tm,),
        in_specs=[pl.BlockSpec((tm, D_MODEL), lambda i: (i, 0)),
                  pl.BlockSpec((D_MODEL, _O_END), lambda i: (0, 0)),
                  pl.BlockSpec((_R_END, D_MODEL), lambda i: (0, 0))],
        out_specs=([pl.BlockSpec((tm, wd), lambda i: (i, 0)) for wd in widths]
                   + [pl.BlockSpec((ht, tm), lambda i: (0, i)) for ht in heights]),
        out_shape=([jax.ShapeDtypeStruct((t, wd), dt) for wd, dt in zip(widths, dtypes)]
                   + [jax.ShapeDtypeStruct((ht, t), dt) for ht, dt in zip(heights, hdtypes)]),
        compiler_params=_cparams(("parallel",)),
        name="inproj",
    )(h, w, wt)


_CONV_HALO = 32
_CONV_SUB = 64


def _convmod_kernel(u_ref, w_ref, b_ref, g_ref, beta_ref, o_ref, buf_ref, *, tt):
    @pl.when(pl.program_id(1) == 0)
    def _():
        buf_ref[0:_CONV_HALO, :] = jnp.zeros((_CONV_HALO, CONV_CH), F32)

    u = u_ref[0]
    buf_ref[_CONV_HALO:_CONV_HALO + tt, :] = u[:, :CONV_CH] * _sigmoid(u[:, CONV_CH:])
    w = w_ref[...]
    base = _CONV_HALO - (CONV_K - 1)
    for r in range(tt // _CONV_SUB):
        acc = jnp.broadcast_to(b_ref[...], (_CONV_SUB, CONV_CH))
        for k in range(CONV_K):
            s0 = r * _CONV_SUB + base + k
            acc = acc + w[k:k + 1, :] * buf_ref[s0:s0 + _CONV_SUB, :]
        y = _layer_norm(acc, g_ref[...], beta_ref[...])
        o_ref[0, r * _CONV_SUB:(r + 1) * _CONV_SUB, :] = (y * _sigmoid(y)).astype(BF16)
    buf_ref[0:_CONV_HALO, :] = buf_ref[tt:tt + _CONV_HALO, :]


def _convmod(u, w, b, g, beta, tt=512):
    bsz, s, _ = u.shape
    vec = pl.BlockSpec((1, CONV_CH), lambda bi, ti: (0, 0))
    return pl.pallas_call(
        functools.partial(_convmod_kernel, tt=tt),
        grid=(bsz, s // tt),
        in_specs=[pl.BlockSpec((1, tt, 2 * CONV_CH), lambda bi, ti: (bi, ti, 0)),
                  pl.BlockSpec((CONV_K + 1, CONV_CH), lambda bi, ti: (0, 0)), vec, vec, vec],
        out_specs=pl.BlockSpec((1, tt, CONV_CH), lambda bi, ti: (bi, ti, 0)),
        out_shape=jax.ShapeDtypeStruct((bsz, s, CONV_CH), BF16),
        scratch_shapes=[pltpu.VMEM((tt + _CONV_HALO, CONV_CH), F32)],
        compiler_params=_cparams(("parallel", "arbitrary")),
        name="convmod",
    )(u, w, b, g, beta)


def _gelu_tanh(x):
    return 0.5 * x * (1.0 + jnp.tanh(0.7978845608028654 * (x + 0.044715 * (x * x * x))))


def _compress_kernel(x_ref, pe_ref, wt_ref, wb_ref, w2_ref, o_ref, *, feature_major):
    x = x_ref[0].astype(F32)
    n = x.shape[0]
    top = _dot((x + pe_ref[0:1, :]).astype(BF16), wt_ref[...])
    bot = _dot((x + pe_ref[1:2, :]).astype(BF16), wb_ref[...])
    hid = top + pltpu.roll(bot, n - 1, 0)
    act = _gelu_tanh(hid).astype(BF16)
    if feature_major:
        o_ref[0] = _dot_nt(w2_ref[...], act).astype(BF16)
    else:
        o_ref[0] = _dot(act, w2_ref[...]).astype(BF16)


def _compress(xr, pe2, wt, wb, w2, feature_major):
    bsz, n, width = xr.shape
    hid = wt.shape[1]
    full = lambda shape: pl.BlockSpec(shape, lambda bi: (0,) * len(shape))
    oshape = (LANES, n) if feature_major else (n, LANES)
    return pl.pallas_call(
        functools.partial(_compress_kernel, feature_major=feature_major),
        grid=(bsz,),
        in_specs=[pl.BlockSpec((1, n, width), lambda bi: (bi, 0, 0)),
                  full((2, width)), full((width, hid)), full((width, hid)), full(w2.shape)],
        out_specs=pl.BlockSpec((1,) + oshape, lambda bi: (bi, 0, 0)),
        out_shape=jax.ShapeDtypeStruct((bsz,) + oshape, BF16),
        compiler_params=_cparams(("parallel",)),
        name="nsa_compress",
    )(xr, pe2, wt, wb, w2)


_TQ = 128
_TK = 512


def _nsa_kernel(qt_ref, kk_ref, vt_ref, gt_ref, kc_ref, vct_ref, mt_ref, o_ref,
                sc_ref, cnt_ref, selb_ref, m_ref, acc_ref, *, s_len):
    tq = _TQ
    cols = NSA_HPG * tq
    n_cmp = s_len // NSA_CMP_STRIDE - 1
    n_sel = s_len // NSA_SEL_LEN
    n_grp = n_sel // SUBLANES
    t0 = pl.program_id(1) * tq
    qt = qt_ref[...] * jnp.asarray(ATTN_SCALE, BF16)
    row_grp = lax.broadcasted_iota(jnp.int32, (LANES, tq), 0) // HEAD_DIM
    keep = [jnp.where(row_grp == g, 1.0, 0.0).astype(BF16) for g in range(NSA_GROUPS)]

    def stack_q(g):
        return jnp.concatenate([keep[g] * qt[hh * LANES:(hh + 1) * LANES, :] for hh in range(NSA_HPG)], axis=1)

    qg = [stack_q(g) for g in range(NSA_GROUPS)]

    def per_head(s, fn):
        return jnp.concatenate([fn(s[:, hh * tq:(hh + 1) * tq]) for hh in range(NSA_HPG)], axis=1)

    def own_rows(v_t, g):
        kb = jnp.where(lax.broadcasted_iota(jnp.int32, v_t.shape, 0) // HEAD_DIM == g, 1.0, 0.0).astype(BF16)
        return kb * v_t + (1.0 - kb)

    def normalise(acc, g):
        oth = (1 - g) * HEAD_DIM
        return acc * (1.0 / acc[oth:oth + 1, :])

    kc = kc_ref[0]
    vct = vct_ref[0]
    ncp = kc.shape[0]
    n_i = lax.broadcasted_iota(jnp.int32, (ncp, tq), 0)
    pos1 = t0 + lax.broadcasted_iota(jnp.int32, (ncp, tq), 1)
    ok1 = (n_i * NSA_CMP_STRIDE + (NSA_CMP_LEN - 1) <= pos1) & (n_i < n_cmp)
    bias1 = jnp.where(ok1, 0.0, MASK_VALUE)
    live1 = jnp.where(ok1, 1.0, 0.0)
    jblk = lax.broadcasted_iota(jnp.int32, (n_sel, tq), 0)
    blk = (t0 + lax.broadcasted_iota(jnp.int32, (n_sel, tq), 1)) >> 6
    causal = jblk <= blk
    forced = (jblk == 0) | (jblk == blk) | (jblk == blk - 1)
    n_live_blocks = (t0 + tq) // NSA_SEL_LEN
    jrow = lax.broadcasted_iota(jnp.int32, (SUBLANES, tq), 0)
    o_cmp = []
    for g in range(NSA_GROUPS):
        s1 = per_head(_dot(kc, qg[g]), lambda x: x + bias1)
        mx = jnp.max(s1, axis=0, keepdims=True)
        p1 = per_head(jnp.exp(s1 - mx), lambda x: x * live1)
        den = jnp.sum(p1, axis=0, keepdims=True)
        p1 = p1 * (1.0 / jnp.where(den > 0.0, den, 1.0))
        o_cmp.append(_dot(vct, p1.astype(BF16)))
        psum = p1[:, 0:tq] + p1[:, tq:2 * tq] + p1[:, 2 * tq:3 * tq] + p1[:, 3 * tq:4 * tq]
        p_hi = psum.astype(BF16)
        p_lo = (psum - p_hi.astype(F32)).astype(BF16)
        imp = _dot(mt_ref[...], p_hi) + _dot(mt_ref[...], p_lo)
        sc_ref[...] = jnp.where(causal, jnp.where(forced, FORCE_SCORE, imp), -jnp.inf)
        cnt_ref[...] = jnp.zeros((n_sel, tq), F32)
        for kb in range(n_grp):
            @pl.when(kb * SUBLANES < n_live_blocks)
            def _(kb=kb):
                sc = sc_ref[...]
                grp = [sc[j * SUBLANES:(j + 1) * SUBLANES, :] for j in range(n_grp)]
                cnt = [cnt_ref[j * SUBLANES:(j + 1) * SUBLANES, :] for j in range(n_grp)]
                for k in range(kb * SUBLANES, (kb + 1) * SUBLANES):
                    rk = sc[k:k + 1, :]
                    for j in range(n_grp):
                        if j > kb:
                            inc = jnp.where(rk >= grp[j], 1.0, 0.0)
                        elif j < kb:
                            inc = jnp.where(rk > grp[j], 1.0, 0.0)
                        else:
                            tie = jnp.where(jrow + j * SUBLANES > k, 1.0, 0.0)
                            inc = jnp.where(rk > grp[j], 1.0, jnp.where(rk == grp[j], tie, 0.0))
                        cnt[j] = cnt[j] + inc
                for j in range(n_grp):
                    cnt_ref[j * SUBLANES:(j + 1) * SUBLANES, :] = cnt[j]
        selb_ref[g] = jnp.where((cnt_ref[...] < NSA_TOP) & causal, 0.0, MASK_VALUE)

    n_chunks = (t0 + tq + _TK - 1) // _TK
    blocks_per_chunk = _TK // NSA_SEL_LEN
    krow = lax.broadcasted_iota(jnp.int32, (_TK, tq), 0)
    posq = t0 + lax.broadcasted_iota(jnp.int32, (_TK, tq), 1)
    m_ref[...] = jnp.full((NSA_GROUPS, 1, cols), MASK_VALUE, F32)
    acc_ref[...] = jnp.zeros((NSA_GROUPS, LANES, cols), F32)

    def chunk(i, carry):
        k0 = pl.multiple_of(i * _TK, _TK)
        ks = kk_ref[0, pl.ds(k0, _TK), 0:LANES]
        vs = vt_ref[0:LANES, pl.ds(k0, _TK)]
        causal2 = krow + k0 <= posq
        raw = [_dot(ks, qg[g]) for g in range(NSA_GROUPS)]
        for g in range(NSA_GROUPS):
            rows = [jnp.broadcast_to(selb_ref[g, pl.ds(i * blocks_per_chunk + j, 1), :], (NSA_SEL_LEN, tq))
                    for j in range(blocks_per_chunk)]
            bias = jnp.where(causal2, jnp.concatenate(rows, axis=0), MASK_VALUE)
            s2 = per_head(raw[g], lambda x: x + bias)
            m_old = m_ref[g]
            m_new = jnp.maximum(m_old, jnp.max(s2, axis=0, keepdims=True))
            p = jnp.exp(s2 - m_new).astype(BF16)
            acc_ref[g] = jnp.exp(m_old - m_new) * acc_ref[g] + _dot(own_rows(vs, g), p)
            m_ref[g] = m_new
        return carry

    lax.fori_loop(0, n_chunks, chunk, 0)
    o_slc = [normalise(acc_ref[g], g) for g in range(NSA_GROUPS)]

    wk = tq + NSA_WINDOW
    w0 = pl.multiple_of(jnp.maximum(t0 - NSA_WINDOW, 0), tq)
    kw = kk_ref[0, pl.ds(w0, wk), LANES:2 * LANES]
    vw_t = vt_ref[LANES:2 * LANES, pl.ds(w0, wk)]
    kpos = w0 + lax.broadcasted_iota(jnp.int32, (wk, tq), 0)
    pos3 = t0 + lax.broadcasted_iota(jnp.int32, (wk, tq), 1)
    bias3 = jnp.where((kpos <= pos3) & (pos3 - kpos < NSA_WINDOW), 0.0, MASK_VALUE)
    o_win = []
    for g in range(NSA_GROUPS):
        s3 = per_head(_dot(kw, qg[g]), lambda x: x + bias3)
        p3 = jnp.exp(s3 - jnp.max(s3, axis=0, keepdims=True)).astype(BF16)
        o_win.append(normalise(_dot(own_rows(vw_t, g), p3), g))

    gates = _sigmoid(gt_ref[...])
    first = row_grp == 0
    for hh in range(NSA_HPG):
        y = jnp.zeros((LANES, tq), F32)
        for br, o in enumerate((o_cmp, o_slc, o_win)):
            r_a, r_b = hh * 3 + br, (NSA_HPG + hh) * 3 + br
            y = y + jnp.where(first, gates[r_a:r_a + 1, :] * o[0][:, hh * tq:(hh + 1) * tq],
                              gates[r_b:r_b + 1, :] * o[1][:, hh * tq:(hh + 1) * tq])
        o_ref[0, :, hh * LANES:(hh + 1) * LANES] = y.T.astype(BF16)


def _nsa_consts(s_len):
    n_cmp_pad = s_len // NSA_CMP_STRIDE
    n_cmp = n_cmp_pad - 1
    n_sel = s_len // NSA_SEL_LEN
    cmp_start = np.arange(n_cmp) * NSA_CMP_STRIDE
    cmp_end = cmp_start + NSA_CMP_LEN - 1
    sel_start = np.arange(n_sel) * NSA_SEL_LEN
    m = ((cmp_start[:, None] <= sel_start[None, :] + NSA_SEL_LEN - 1) & (cmp_end[:, None] >= sel_start[None, :]))
    mt = np.zeros((n_sel, n_cmp_pad), np.float32)
    mt[:, :n_cmp] = m.T
    return jnp.asarray(mt, BF16)


def _nsa_attention(qt, kk, vt, gt, kc, vct, mt, bsz, s):
    assert s % _TK == 0 and s >= _TQ + NSA_WINDOW and (s // NSA_SEL_LEN) % SUBLANES == 0
    ncp = kc.shape[1]
    n_sel = s // NSA_SEL_LEN
    cols = NSA_HPG * _TQ
    nq = s // _TQ
    return pl.pallas_call(
        functools.partial(_nsa_kernel, s_len=s),
        grid=(bsz, nq),
        in_specs=[pl.BlockSpec((4 * LANES, _TQ), lambda bi, ci: (0, bi * nq + ci)),
                  pl.BlockSpec((1, s, 2 * LANES), lambda bi, ci: (bi, 0, 0)),
                  pl.BlockSpec((2 * LANES, s), lambda bi, ci: (0, bi)),
                  pl.BlockSpec((LANES, _TQ), lambda bi, ci: (0, bi * nq + ci)),
                  pl.BlockSpec((1, ncp, LANES), lambda bi, ci: (bi, 0, 0)),
                  pl.BlockSpec((1, LANES, ncp), lambda bi, ci: (bi, 0, 0)),
                  pl.BlockSpec(mt.shape, lambda bi, ci: (0, 0))],
        out_specs=pl.BlockSpec((1, _TQ, 4 * LANES), lambda bi, ci: (bi, ci, 0)),
        out_shape=jax.ShapeDtypeStruct((bsz, s, 4 * LANES), BF16),
        scratch_shapes=[pltpu.VMEM((n_sel, _TQ), F32), pltpu.VMEM((n_sel, _TQ), F32),
                        pltpu.VMEM((NSA_GROUPS, n_sel, _TQ), F32),
                        pltpu.VMEM((NSA_GROUPS, 1, cols), F32), pltpu.VMEM((NSA_GROUPS, LANES, cols), F32)],
        compiler_params=_cparams(("parallel", "arbitrary")),
        name="nsa_attention",
    )(qt, kk, vt, gt, kc, vct, mt)


_HC = 64


def _hgrn_consts():
    c = _HC
    t = np.arange(c)[:, None]
    u = np.arange(c)[None, :]
    sets = [u <= t,
            u > t,
            (u >= (t // 16) * 16) & (u <= t),
            (u >= (t // 4) * 4) & (u <= t)]
    for m in (1, 2, 3):
        sets.append((u > t) & (u <= 16 * m - 1))
    for m in (1, 2, 3):
        sets.append((u > t) & (u <= (t // 16) * 16 + 4 * m - 1))
    for m in (1, 2, 3):
        sets.append((u > t) & (u <= (t // 4) * 4 + m))
    dmat = np.concatenate([x.astype(np.float32) for x in sets], axis=0)

    r = np.arange(4 * c)[:, None] % c
    col = np.arange(4 * c)[None, :]
    slot, s = col // c, col % c
    m16 = (slot == r // 16 - 1) & (s // 16 < r // 16)
    m4 = (slot == (r % 16) // 4 - 1) & (s // 16 == r // 16) & ((s % 16) // 4 < (r % 16) // 4)
    m1 = (slot == r % 4) & (s // 4 == r // 4) & (s % 4 <= r % 4)
    masks = np.stack([m16, m4, m1]).astype(np.float32)
    hr = np.arange(4 * c)[:, None] // c
    hc = np.arange(4 * c)[None, :] // c
    bd = (hr == hc).astype(np.float32)
    return jnp.asarray(dmat, BF16), jnp.asarray(masks, F32), jnp.asarray(bd, F32)


def _hgrn_kernel(u_ref, lbl_ref, ng_ref, d_ref, mk_ref, bd_ref, o_ref, st_ref, *, layer, rb):
    w = HGRN_W
    c = _HC

    @pl.when(pl.program_id(1) == 0)
    def _():
        st_ref[...] = jnp.zeros((w, w), F32)

    lg = [lbl_ref[r:r + 1, :] for r in range(DEPTH)]
    mx = functools.reduce(jnp.maximum, lg)
    ex = [jnp.exp(x - mx) for x in lg]
    tot = functools.reduce(lambda a, b: a + b, ex)
    lb = jnp.zeros((1, w), F32)
    for r in range(1, layer + 1):
        lb = lb + ex[r] / tot

    bd = bd_ref[...]
    lane_head = lax.broadcasted_iota(jnp.int32, (c, w), 1) // HEAD_DIM
    head_keep = [lane_head == h for h in range(HGRN_HEADS)]
    head_keep_b = [jnp.where(m, 1.0, 0.0).astype(BF16) for m in head_keep]

    def stack_heads(x):
        return jnp.concatenate([head_keep_b[h] * x for h in range(HGRN_HEADS)], axis=0)

    def chunk(ci, carry):
        r0 = pl.multiple_of(ci * c, c)
        q = u_ref[0, pl.ds(r0, c), 0:w]
        fz = u_ref[0, pl.ds(r0, c), w:2 * w]
        v = u_ref[0, pl.ds(r0, c), 2 * w:3 * w]
        gz = u_ref[0, pl.ds(r0, c), 3 * w:4 * w]
        f = lb + (1.0 - lb) * _sigmoid(fz)
        logf = jnp.log(f)
        kk = 1.0 - f
        lhi = logf.astype(BF16)
        llo = (logf - lhi.astype(F32)).astype(BF16)
        ex_all = jnp.exp(_dot(d_ref[...], lhi) + _dot(d_ref[...], llo))

        def eset(i):
            return ex_all[i * c:(i + 1) * c, :]

        v_b = v.astype(BF16)
        kq = [q * eset(2), q * eset(3), q]
        zero_k = jnp.zeros_like(kk)
        kk_sets = [[kk * eset(4 + j) for j in range(3)] + [zero_k],
                   [kk * eset(7 + j) for j in range(3)] + [zero_k],
                   [kk] + [kk * eset(10 + j) for j in range(3)]]
        sw = jnp.zeros((HGRN_HEADS * c, HGRN_HEADS * c), F32)
        for lvl in range(3):
            lhs = stack_heads(kq[lvl].astype(BF16))
            rhs = jnp.concatenate([x.astype(BF16) for x in kk_sets[lvl]], axis=0)
            sw = sw + _dot_nt(lhs, rhs) * mk_ref[lvl]
        v_rep = jnp.concatenate([v_b] * HGRN_HEADS, axis=0)
        o_full = _dot(sw.astype(BF16), v_rep)
        o = jnp.zeros((c, w), F32)
        for h in range(HGRN_HEADS):
            o = o + jnp.where(head_keep[h], o_full[h * c:(h + 1) * c, :], 0.0)
        st = st_ref[...]
        o = o + _dot_nt((q * eset(0)).astype(BF16), st.astype(BF16))
        kst = (kk * eset(1)).astype(BF16)
        upd = lax.dot_general(v_b, kst, _TN, preferred_element_type=F32)
        st_ref[...] = ex_all[c - 1:c, :] * st + bd * upd
        ms = _split_dot(o * o, bd_ref[...].astype(BF16)) * (1.0 / HEAD_DIM)
        y = o * lax.rsqrt(ms + LN_EPS) * ng_ref[...]
        y = y * (gz * _sigmoid(gz))
        o_ref[0, pl.ds(r0, c), :] = y.astype(BF16)
        return carry

    lax.fori_loop(0, rb // c, chunk, 0)


def _hgrn(u, lb_logits, norm_g4, consts, layer, rb=512):
    bsz, s, _ = u.shape
    dmat, masks, bd = consts
    full = lambda arr: pl.BlockSpec(arr.shape, lambda bi, ci: (0,) * arr.ndim)
    return pl.pallas_call(
        functools.partial(_hgrn_kernel, layer=layer, rb=rb),
        grid=(bsz, s // rb),
        in_specs=[pl.BlockSpec((1, rb, 4 * HGRN_W), lambda bi, ci: (bi, ci, 0)),
                  full(lb_logits), full(norm_g4), full(dmat), full(masks), full(bd)],
        out_specs=pl.BlockSpec((1, rb, HGRN_W), lambda bi, ci: (bi, ci, 0)),
        out_shape=jax.ShapeDtypeStruct((bsz, s, HGRN_W), BF16),
        scratch_shapes=[pltpu.VMEM((HGRN_W, HGRN_W), F32)],
        compiler_params=_cparams(("parallel", "arbitrary")),
        name="hgrn2",
    )(u, lb_logits, norm_g4, dmat, masks, bd)


def _outproj_kernel(h_ref, yc_ref, yn_ref, yh_ref, wc_ref, wn_ref, wh_ref, g_ref, b_ref, o_ref):
    m = _dot(yc_ref[...], wc_ref[...]) + _dot(yn_ref[...], wn_ref[...]) + _dot(yh_ref[...], wh_ref[...])
    o_ref[...] = _layer_norm(DN_ALPHA * h_ref[...] + m, g_ref[...], b_ref[...])


def _outproj(h, yc, yn, yh, wc, wn, wh, g, b, tm=512):
    t = h.shape[0]
    row = lambda wd: pl.BlockSpec((tm, wd), lambda i: (i, 0))
    full = lambda arr: pl.BlockSpec(arr.shape, lambda i: (0,) * arr.ndim)
    return pl.pallas_call(
        _outproj_kernel,
        grid=(t // tm,),
        in_specs=[row(D_MODEL), row(CONV_CH), row(4 * LANES), row(HGRN_W),
                  full(wc), full(wn), full(wh), full(g), full(b)],
        out_specs=row(D_MODEL),
        out_shape=jax.ShapeDtypeStruct((t, D_MODEL), F32),
        compiler_params=_cparams(("parallel",)),
        name="outproj_ln1",
    )(h, yc, yn, yh, wc, wn, wh, g, b)


_FF_HALO = 8


def _ffn_kernel(h_ref, halo_ref, p_ref, wv_ref, wg_ref, cwv_ref, cwg_ref, cbv_ref, cbg_ref, wdn_ref, wpg_ref, wpp_ref,
                g_ref, b_ref, o_ref, acc_ref, *, tm, s_len):
    i = pl.program_id(0)
    j = pl.program_id(1)
    x = h_ref[...].astype(BF16)

    @pl.when(j == 0)
    def _():
        gate = _sigmoid(_dot(x, wpg_ref[...]))
        acc_ref[...] = gate * _dot(p_ref[...].astype(BF16), wpp_ref[...])

    seq_start = (i * tm) % s_len == 0
    halo = jnp.where(seq_start, 0.0, halo_ref[...]).astype(BF16)
    xh = jnp.concatenate([halo, x], axis=0)

    def conv_up(w_ref, cw_ref, cb_ref):
        up = _dot(xh, w_ref[...])
        cw = cw_ref[...]
        return (cw[0:1, :] * up[_FF_HALO - 2:_FF_HALO - 2 + tm] + cw[1:2, :] * up[_FF_HALO - 1:_FF_HALO - 1 + tm]
                + cw[2:3, :] * up[_FF_HALO:_FF_HALO + tm] + cb_ref[...])

    val = conv_up(wv_ref, cwv_ref, cbv_ref)
    gate = conv_up(wg_ref, cwg_ref, cbg_ref)
    act = (gate * _sigmoid(gate) * val).astype(BF16)
    acc_ref[...] += _dot(act, wdn_ref[...])

    @pl.when(j == pl.num_programs(1) - 1)
    def _():
        o_ref[...] = _layer_norm(DN_ALPHA * h_ref[...] + acc_ref[...], g_ref[...], b_ref[...])


def _ffn(h, p, wup, cw, cb, wdn, wpg, wpp, g, b, s_len, tm=512, nf=2):
    t = h.shape[0]
    tf = D_FF // nf
    hb = tm // _FF_HALO
    full = lambda arr: pl.BlockSpec(arr.shape, lambda i, j: (0,) * arr.ndim)
    half = lambda rows, off: pl.BlockSpec((rows, tf), lambda i, j: (0, off + j))
    return pl.pallas_call(
        functools.partial(_ffn_kernel, tm=tm, s_len=s_len),
        grid=(t // tm, nf),
        in_specs=[pl.BlockSpec((tm, D_MODEL), lambda i, j: (i, 0)),
                  pl.BlockSpec((_FF_HALO, D_MODEL), lambda i, j: (jnp.maximum(i * hb - 1, 0), 0)),
                  pl.BlockSpec((tm, D_PLE), lambda i, j: (i, 0)),
                  half(D_MODEL, 0), half(D_MODEL, nf), half(SUBLANES, 0), half(SUBLANES, nf), half(1, 0), half(1, nf),
                  pl.BlockSpec((tf, D_MODEL), lambda i, j: (j, 0)),
                  full(wpg), full(wpp), full(g), full(b)],
        out_specs=pl.BlockSpec((tm, D_MODEL), lambda i, j: (i, 0)),
        out_shape=jax.ShapeDtypeStruct((t, D_MODEL), F32),
        scratch_shapes=[pltpu.VMEM((tm, D_MODEL), F32)],
        compiler_params=_cparams(("parallel", "arbitrary")),
        name="ffn_ple_ln2",
    )(h, h, p, wup, wup, cw, cw, cb, cb, wdn, wpg, wpp, g, b)


def _pack_in_weights(w_in):
    kv0 = 1024
    col = lambda k: w_in[:, :, kv0 + k * LANES:kv0 + (k + 1) * LANES]
    w_nat = jnp.concatenate([w_in[:, :, 0:512], col(0), col(1), col(2), col(4), w_in[:, :, 1816:2840]], axis=-1)
    qw = w_in[:, :, 512:1024].reshape(DEPTH, D_MODEL, NSA_GROUPS, NSA_HPG, HEAD_DIM)
    qw = qw.transpose(0, 1, 3, 2, 4).reshape(DEPTH, D_MODEL, 512)
    gw = jnp.pad(w_in[:, :, 1792:1816], ((0, 0), (0, 0), (0, LANES - 24)))
    w_t = jnp.concatenate([qw, col(3), col(5), gw], axis=-1).transpose(0, 2, 1)
    return w_nat.astype(BF16), w_t.astype(BF16)


def _pack_cmp_weights(pe, w1, w2):
    d = pe.shape[0]
    eye = jnp.eye(NSA_GROUPS, dtype=F32)
    pe2 = jnp.broadcast_to(pe.reshape(d, 2, 16, 1, HEAD_DIM), (d, 2, 16, NSA_GROUPS, HEAD_DIM)).reshape(d, 2, 2048)
    w1r = w1.reshape(d, 2, 16, HEAD_DIM, NSA_CMP_HIDDEN)
    w1e = jnp.einsum('dplkh,ge->dplgkeh', w1r, eye).reshape(d, 2, 2048, NSA_GROUPS * NSA_CMP_HIDDEN)
    w2e = jnp.einsum('dhk,ge->dghek', w2, eye).reshape(d, NSA_GROUPS * NSA_CMP_HIDDEN, LANES)
    return pe2, w1e[:, 0].astype(BF16), w1e[:, 1].astype(BF16), w2e.astype(BF16), w2e.transpose(0, 2, 1).astype(BF16)


def _pack_out_weights(w_out):
    wc = w_out[:, 0:CONV_CH]
    wn = w_out[:, CONV_CH:CONV_CH + 512].reshape(DEPTH, NSA_GROUPS, NSA_HPG, HEAD_DIM, D_MODEL)
    wn = wn.transpose(0, 2, 1, 3, 4).reshape(DEPTH, 512, D_MODEL)
    wh = w_out[:, CONV_CH + 512:]
    return wc.astype(BF16), wn.astype(BF16), wh.astype(BF16)


def _pack_ffn_weights(w_up, conv_w, conv_b, w_down):
    cw = jnp.pad(conv_w, ((0, 0), (0, SUBLANES - conv_w.shape[1]), (0, 0)))
    return w_up.astype(BF16), cw, conv_b[:, None, :], w_down.astype(BF16)


def kernel(x, p, w_in, conv_w, conv_b, conv_ln_g, conv_ln_b, cmp_pe_k, cmp_pe_v, cmp_w1_k, cmp_w2_k, cmp_w1_v,
           cmp_w2_v, lb_logits, hgrn_norm_g, w_out, ln1_g, ln1_b, w_up, ffn_conv_w, ffn_conv_b, w_down,
           w_ple_gate, w_ple_proj, ln2_g, ln2_b):
    bsz, s, _ = x.shape
    t = bsz * s
    w_nat, w_t = _pack_in_weights(w_in)
    pe_k, w1k_t, w1k_b, w2k, _ = _pack_cmp_weights(cmp_pe_k, cmp_w1_k, cmp_w2_k)
    pe_v, w1v_t, w1v_b, _, w2v_t = _pack_cmp_weights(cmp_pe_v, cmp_w1_v, cmp_w2_v)
    wc, wn, wh = _pack_out_weights(w_out)
    wup, fcw, fcb, wdn = _pack_ffn_weights(w_up, ffn_conv_w, ffn_conv_b, w_down)
    conv_w_p = jnp.pad(conv_w, ((0, 0), (0, 1), (0, 0)))
    norm_g4 = jnp.tile(hgrn_norm_g, (1, HGRN_HEADS))[:, None, :]
    wpg = w_ple_gate.astype(BF16)
    wpp = w_ple_proj.astype(BF16)
    mt = _nsa_consts(s)
    hgrn_consts = _hgrn_consts()
    n_rows = s // NSA_CMP_STRIDE

    h = x.reshape(t, D_MODEL)
    for i in range(DEPTH):
        conv_u, kcu, vcu, kk, hg, qt, vt, gt = _inproj(h, w_nat[i], w_t[i])
        y_conv = _convmod(conv_u.reshape(bsz, s, 2 * CONV_CH), conv_w_p[i], conv_b[i][None], conv_ln_g[i][None],
                          conv_ln_b[i][None])
        kc = _compress(kcu.reshape(bsz, n_rows, NSA_CMP_STRIDE * LANES), pe_k[i], w1k_t[i], w1k_b[i], w2k[i], False)
        vct = _compress(vcu.reshape(bsz, n_rows, NSA_CMP_STRIDE * LANES), pe_v[i], w1v_t[i], w1v_b[i], w2v_t[i], True)
        y_nsa = _nsa_attention(qt, kk.reshape(bsz, s, 2 * LANES), vt, gt, kc, vct, mt, bsz, s)
        y_hgrn = _hgrn(hg.reshape(bsz, s, 4 * HGRN_W), lb_logits, norm_g4[i], hgrn_consts, i)
        h1 = _outproj(h, y_conv.reshape(t, CONV_CH), y_nsa.reshape(t, 512), y_hgrn.reshape(t, HGRN_W),
                      wc[i], wn[i], wh[i], ln1_g[i][None], ln1_b[i][None])
        h = _ffn(h1, p[i].reshape(t, D_PLE), wup[i], fcw[i], fcb[i], wdn[i], wpg[i], wpp[i],
                 ln2_g[i][None], ln2_b[i][None], s)
    return h.reshape(bsz, s, D_MODEL)
```

```python
import functools

import numpy as np
import jax
import jax.numpy as jnp
from jax import lax
from jax.experimental import pallas as pl
from jax.experimental.pallas import tpu as pltpu

F32 = jnp.float32
BF16 = jnp.bfloat16

D_MODEL = 1024
DEPTH = 4
D_PLE = 256
HEAD_DIM = 64
CONV_CH = 256
CONV_K = 31
NSA_HEADS = 8
NSA_GROUPS = 2
NSA_HPG = NSA_HEADS // NSA_GROUPS
NSA_CMP_STRIDE = 16
NSA_CMP_LEN = 32
NSA_CMP_HIDDEN = 128
NSA_SEL_LEN = 64
NSA_TOP = 16
NSA_WINDOW = 512
HGRN_HEADS = 4
HGRN_W = HGRN_HEADS * HEAD_DIM
D_FF = 2816
DN_ALPHA = (2 * DEPTH) ** 0.25
ATTN_SCALE = HEAD_DIM ** -0.5
LN_EPS = 1e-5
MASK_VALUE = -1e30
FORCE_SCORE = 1e9

LANES = 128
SUBLANES = 8
VMEM_LIMIT = 56 * 1024 * 1024

_O_CONV, _O_KC, _O_VC, _O_KK, _O_HG, _O_END = 0, 512, 640, 768, 1024, 2048
_R_Q, _R_V, _R_GATE, _R_END = 0, 512, 768, 896

_NT = (((1,), (1,)), ((), ()))
_TN = (((0,), (0,)), ((), ()))


def _cparams(sem):
    return pltpu.CompilerParams(dimension_semantics=sem, vmem_limit_bytes=VMEM_LIMIT)


def _dot(a, b):
    return jnp.dot(a, b, preferred_element_type=F32)


def _dot_nt(a, b):
    return lax.dot_general(a, b, _NT, preferred_element_type=F32)


def _split_dot(x, w):
    hi = x.astype(BF16)
    lo = (x - hi.astype(F32)).astype(BF16)
    return _dot(hi, w) + _dot(lo, w)


def _sigmoid(x):
    return 1.0 / (1.0 + jnp.exp(-x))


def _layer_norm(z, g, b):
    mu = jnp.mean(z, axis=-1, keepdims=True)
    zc = z - mu
    var = jnp.mean(zc * zc, axis=-1, keepdims=True)
    return zc * lax.rsqrt(var + LN_EPS) * g + b


def _inproj_kernel(h_ref, w_ref, wt_ref, conv_ref, kc_ref, vc_ref, kk_ref, hg_ref, qt_ref, vt_ref, gt_ref):
    x = h_ref[...].astype(BF16)
    conv_ref[...] = _dot(x, w_ref[:, _O_CONV:_O_KC])
    kc_ref[...] = _dot(x, w_ref[:, _O_KC:_O_VC]).astype(BF16)
    vc_ref[...] = _dot(x, w_ref[:, _O_VC:_O_KK]).astype(BF16)
    kk_ref[...] = _dot(x, w_ref[:, _O_KK:_O_HG]).astype(BF16)
    hg_ref[...] = _dot(x, w_ref[:, _O_HG:_O_END])
    qt_ref[...] = _dot_nt(wt_ref[_R_Q:_R_V, :], x).astype(BF16)
    vt_ref[...] = _dot_nt(wt_ref[_R_V:_R_GATE, :], x).astype(BF16)
    gt_ref[...] = _dot_nt(wt_ref[_R_GATE:_R_END, :], x)


def _inproj(h, w, wt, tm=512):
    t = h.shape[0]
    widths = (512, 128, 128, 256, 1024)
    dtypes = (F32, BF16, BF16, BF16, F32)
    heights = (512, 256, 128)
    hdtypes = (BF16, BF16, F32)
    return pl.pallas_call(
        _inproj_kernel,
        grid=(t // tm,),
        in_specs=[pl.BlockSpec((tm, D_MODEL), lambda i: (i, 0)),
                  pl.BlockSpec((D_MODEL, _O_END), lambda i: (0, 0)),
                  pl.BlockSpec((_R_END, D_MODEL), lambda i: (0, 0))],
        out_specs=([pl.BlockSpec((tm, wd), lambda i: (i, 0)) for wd in widths]
                   + [pl.BlockSpec((ht, tm), lambda i: (0, i)) for ht in heights]),
        out_shape=([jax.ShapeDtypeStruct((t, wd), dt) for wd, dt in zip(widths, dtypes)]
                   + [jax.ShapeDtypeStruct((ht, t), dt) for ht, dt in zip(heights, hdtypes)]),
        compiler_params=_cparams(("parallel",)),
        name="inproj",
    )(h, w, wt)


_CONV_HALO = 32
_CONV_SUB = 64


def _convmod_kernel(u_ref, w_ref, b_ref, g_ref, beta_ref, o_ref, buf_ref, *, tt):
    @pl.when(pl.program_id(1) == 0)
    def _():
        buf_ref[0:_CONV_HALO, :] = jnp.zeros((_CONV_HALO, CONV_CH), F32)

    u = u_ref[0]
    buf_ref[_CONV_HALO:_CONV_HALO + tt, :] = u[:, :CONV_CH] * _sigmoid(u[:, CONV_CH:])
    w = w_ref[...]
    base = _CONV_HALO - (CONV_K - 1)
    for r in range(tt // _CONV_SUB):
        acc = jnp.broadcast_to(b_ref[...], (_CONV_SUB, CONV_CH))
        for k in range(CONV_K):
            s0 = r * _CONV_SUB + base + k
            acc = acc + w[k:k + 1, :] * buf_ref[s0:s0 + _CONV_SUB, :]
        y = _layer_norm(acc, g_ref[...], beta_ref[...])
        o_ref[0, r * _CONV_SUB:(r + 1) * _CONV_SUB, :] = (y * _sigmoid(y)).astype(BF16)
    buf_ref[0:_CONV_HALO, :] = buf_ref[tt:tt + _CONV_HALO, :]


def _convmod(u, w, b, g, beta, tt=512):
    bsz, s, _ = u.shape
    vec = pl.BlockSpec((1, CONV_CH), lambda bi, ti: (0, 0))
    return pl.pallas_call(
        functools.partial(_convmod_kernel, tt=tt),
        grid=(bsz, s // tt),
        in_specs=[pl.BlockSpec((1, tt, 2 * CONV_CH), lambda bi, ti: (bi, ti, 0)),
                  pl.BlockSpec((CONV_K + 1, CONV_CH), lambda bi, ti: (0, 0)), vec, vec, vec],
        out_specs=pl.BlockSpec((1, tt, CONV_CH), lambda bi, ti: (bi, ti, 0)),
        out_shape=jax.ShapeDtypeStruct((bsz, s, CONV_CH), BF16),
        scratch_shapes=[pltpu.VMEM((tt + _CONV_HALO, CONV_CH), F32)],
        compiler_params=_cparams(("parallel", "arbitrary")),
        name="convmod",
    )(u, w, b, g, beta)


def _gelu_tanh(x):
    return 0.5 * x * (1.0 + jnp.tanh(0.7978845608028654 * (x + 0.044715 * (x * x * x))))


def _compress_kernel(x_ref, pe_ref, wt_ref, wb_ref, w2_ref, o_ref, *, feature_major):
    x = x_ref[0].astype(F32)
    n = x.shape[0]
    top = _dot((x + pe_ref[0:1, :]).astype(BF16), wt_ref[...])
    bot = _dot((x + pe_ref[1:2, :]).astype(BF16), wb_ref[...])
    hid = top + pltpu.roll(bot, n - 1, 0)
    act = _gelu_tanh(hid).astype(BF16)
    if feature_major:
        o_ref[0] = _dot_nt(w2_ref[...], act).astype(BF16)
    else:
        o_ref[0] = _dot(act, w2_ref[...]).astype(BF16)


def _compress(xr, pe2, wt, wb, w2, feature_major):
    bsz, n, width = xr.shape
    hid = wt.shape[1]
    full = lambda shape: pl.BlockSpec(shape, lambda bi: (0,) * len(shape))
    oshape = (LANES, n) if feature_major else (n, LANES)
    return pl.pallas_call(
        functools.partial(_compress_kernel, feature_major=feature_major),
        grid=(bsz,),
        in_specs=[pl.BlockSpec((1, n, width), lambda bi: (bi, 0, 0)),
                  full((2, width)), full((width, hid)), full((width, hid)), full(w2.shape)],
        out_specs=pl.BlockSpec((1,) + oshape, lambda bi: (bi, 0, 0)),
        out_shape=jax.ShapeDtypeStruct((bsz,) + oshape, BF16),
        compiler_params=_cparams(("parallel",)),
        name="nsa_compress",
    )(xr, pe2, wt, wb, w2)


_TQ = 128
_TK = 512


def _nsa_kernel(qt_ref, kk_ref, vt_ref, gt_ref, kc_ref, vct_ref, mt_ref, o_ref,
                sc_ref, cnt_ref, selb_ref, m_ref, acc_ref, *, s_len):
    tq = _TQ
    cols = NSA_HPG * tq
    n_cmp = s_len // NSA_CMP_STRIDE - 1
    n_sel = s_len // NSA_SEL_LEN
    n_grp = n_sel // SUBLANES
    t0 = pl.program_id(1) * tq
    qt = qt_ref[...] * jnp.asarray(ATTN_SCALE, BF16)
    row_grp = lax.broadcasted_iota(jnp.int32, (LANES, tq), 0) // HEAD_DIM
    keep = [jnp.where(row_grp == g, 1.0, 0.0).astype(BF16) for g in range(NSA_GROUPS)]

    def stack_q(g):
        return jnp.concatenate([keep[g] * qt[hh * LANES:(hh + 1) * LANES, :] for hh in range(NSA_HPG)], axis=1)

    qg = [stack_q(g) for g in range(NSA_GROUPS)]

    def per_head(s, fn):
        return jnp.concatenate([fn(s[:, hh * tq:(hh + 1) * tq]) for hh in range(NSA_HPG)], axis=1)

    def own_rows(v_t, g):
        kb = jnp.where(lax.broadcasted_iota(jnp.int32, v_t.shape, 0) // HEAD_DIM == g, 1.0, 0.0).astype(BF16)
        return kb * v_t + (1.0 - kb)

    def normalise(acc, g):
        oth = (1 - g) * HEAD_DIM
        return acc * (1.0 / acc[oth:oth + 1, :])

    kc = kc_ref[0]
    vct = vct_ref[0]
    ncp = kc.shape[0]
    n_i = lax.broadcasted_iota(jnp.int32, (ncp, tq), 0)
    pos1 = t0 + lax.broadcasted_iota(jnp.int32, (ncp, tq), 1)
    ok1 = (n_i * NSA_CMP_STRIDE + (NSA_CMP_LEN - 1) <= pos1) & (n_i < n_cmp)
    bias1 = jnp.where(ok1, 0.0, MASK_VALUE)
    live1 = jnp.where(ok1, 1.0, 0.0)
    jblk = lax.broadcasted_iota(jnp.int32, (n_sel, tq), 0)
    blk = (t0 + lax.broadcasted_iota(jnp.int32, (n_sel, tq), 1)) >> 6
    causal = jblk <= blk
    forced = (jblk == 0) | (jblk == blk) | (jblk == blk - 1)
    n_live_blocks = (t0 + tq) // NSA_SEL_LEN
    jrow = lax.broadcasted_iota(jnp.int32, (SUBLANES, tq), 0)
    o_cmp = []
    raw1 = [_dot(kc, qg[g]) for g in range(NSA_GROUPS)]
    for g in range(NSA_GROUPS):
        s1 = per_head(raw1[g], lambda x: x + bias1)
        mx = jnp.max(s1, axis=0, keepdims=True)
        p1 = per_head(jnp.exp(s1 - mx), lambda x: x * live1)
        den = jnp.sum(p1, axis=0, keepdims=True)
        p1 = p1 * (1.0 / jnp.where(den > 0.0, den, 1.0))
        o_cmp.append(_dot(vct, p1.astype(BF16)))
        psum = p1[:, 0:tq] + p1[:, tq:2 * tq] + p1[:, 2 * tq:3 * tq] + p1[:, 3 * tq:4 * tq]
        p_hi = psum.astype(BF16)
        p_lo = (psum - p_hi.astype(F32)).astype(BF16)
        imp = _dot(mt_ref[...], p_hi) + _dot(mt_ref[...], p_lo)
        sc_ref[...] = jnp.where(causal, jnp.where(forced, FORCE_SCORE, imp), -jnp.inf)
        cnt_ref[...] = jnp.zeros((n_sel, tq), F32)
        for kb in range(n_grp):
            @pl.when((kb * SUBLANES < n_live_blocks) & (n_live_blocks > NSA_TOP))
            def _(kb=kb):
                sc = sc_ref[...]
                grp = [sc[j * SUBLANES:(j + 1) * SUBLANES, :] for j in range(n_grp)]
                cnt = [cnt_ref[j * SUBLANES:(j + 1) * SUBLANES, :] for j in range(n_grp)]
                for k in range(kb * SUBLANES, (kb + 1) * SUBLANES):
                    rk = sc[k:k + 1, :]
                    for j in range(n_grp):
                        if j > kb:
                            inc = jnp.where(rk >= grp[j], 1.0, 0.0)
                        elif j < kb:
                            inc = jnp.where(rk > grp[j], 1.0, 0.0)
                        else:
                            tie = jnp.where(jrow + j * SUBLANES > k, 1.0, 0.0)
                            inc = jnp.where(rk > grp[j], 1.0, jnp.where(rk == grp[j], tie, 0.0))
                        cnt[j] = cnt[j] + inc
                for j in range(n_grp):
                    cnt_ref[j * SUBLANES:(j + 1) * SUBLANES, :] = cnt[j]
        selb_ref[g] = jnp.where((cnt_ref[...] < NSA_TOP) & causal, 0.0, MASK_VALUE)

    n_chunks = (t0 + tq + _TK - 1) // _TK
    blocks_per_chunk = _TK // NSA_SEL_LEN
    krow = lax.broadcasted_iota(jnp.int32, (_TK, tq), 0)
    posq = t0 + lax.broadcasted_iota(jnp.int32, (_TK, tq), 1)
    m_ref[...] = jnp.full((NSA_GROUPS, 1, cols), MASK_VALUE, F32)
    acc_ref[...] = jnp.zeros((NSA_GROUPS, LANES, cols), F32)

    def chunk(i, carry):
        k0 = pl.multiple_of(i * _TK, _TK)
        ks = kk_ref[0, pl.ds(k0, _TK), 0:LANES]
        vs = vt_ref[0:LANES, pl.ds(k0, _TK)]
        causal2 = krow + k0 <= posq
        raw = [_dot(ks, qg[g]) for g in range(NSA_GROUPS)]
        for g in range(NSA_GROUPS):
            rows = [jnp.broadcast_to(selb_ref[g, pl.ds(i * blocks_per_chunk + j, 1), :], (NSA_SEL_LEN, tq))
                    for j in range(blocks_per_chunk)]
            bias = jnp.where(causal2, jnp.concatenate(rows, axis=0), MASK_VALUE)
            s2 = per_head(raw[g], lambda x: x + bias)
            m_old = m_ref[g]
            m_new = jnp.maximum(m_old, jnp.max(s2, axis=0, keepdims=True))
            p = jnp.exp(s2 - m_new).astype(BF16)
            acc_ref[g] = jnp.exp(m_old - m_new) * acc_ref[g] + _dot(own_rows(vs, g), p)
            m_ref[g] = m_new
        return carry

    lax.fori_loop(0, n_chunks, chunk, 0)
    o_slc = [normalise(acc_ref[g], g) for g in range(NSA_GROUPS)]

    wk = tq + NSA_WINDOW
    w0 = pl.multiple_of(jnp.maximum(t0 - NSA_WINDOW, 0), tq)
    kw = kk_ref[0, pl.ds(w0, wk), LANES:2 * LANES]
    vw_t = vt_ref[LANES:2 * LANES, pl.ds(w0, wk)]
    kpos = w0 + lax.broadcasted_iota(jnp.int32, (wk, tq), 0)
    pos3 = t0 + lax.broadcasted_iota(jnp.int32, (wk, tq), 1)
    bias3 = jnp.where((kpos <= pos3) & (pos3 - kpos < NSA_WINDOW), 0.0, MASK_VALUE)
    o_win = []
    raw3 = [_dot(kw, qg[g]) for g in range(NSA_GROUPS)]
    for g in range(NSA_GROUPS):
        s3 = per_head(raw3[g], lambda x: x + bias3)
        p3 = jnp.exp(s3 - jnp.max(s3, axis=0, keepdims=True)).astype(BF16)
        o_win.append(normalise(_dot(own_rows(vw_t, g), p3), g))

    gates = _sigmoid(gt_ref[...])
    first = row_grp == 0
    for hh in range(NSA_HPG):
        y = jnp.zeros((LANES, tq), F32)
        for br, o in enumerate((o_cmp, o_slc, o_win)):
            r_a, r_b = hh * 3 + br, (NSA_HPG + hh) * 3 + br
            y = y + jnp.where(first, gates[r_a:r_a + 1, :] * o[0][:, hh * tq:(hh + 1) * tq],
                              gates[r_b:r_b + 1, :] * o[1][:, hh * tq:(hh + 1) * tq])
        o_ref[0, :, hh * LANES:(hh + 1) * LANES] = y.T.astype(BF16)


def _nsa_consts(s_len):
    n_cmp_pad = s_len // NSA_CMP_STRIDE
    n_cmp = n_cmp_pad - 1
    n_sel = s_len // NSA_SEL_LEN
    cmp_start = np.arange(n_cmp) * NSA_CMP_STRIDE
    cmp_end = cmp_start + NSA_CMP_LEN - 1
    sel_start = np.arange(n_sel) * NSA_SEL_LEN
    m = ((cmp_start[:, None] <= sel_start[None, :] + NSA_SEL_LEN - 1) & (cmp_end[:, None] >= sel_start[None, :]))
    mt = np.zeros((n_sel, n_cmp_pad), np.float32)
    mt[:, :n_cmp] = m.T
    return jnp.asarray(mt, BF16)


def _nsa_attention(qt, kk, vt, gt, kc, vct, mt, bsz, s):
    assert s % _TK == 0 and s >= _TQ + NSA_WINDOW and (s // NSA_SEL_LEN) % SUBLANES == 0
    ncp = kc.shape[1]
    n_sel = s // NSA_SEL_LEN
    cols = NSA_HPG * _TQ
    nq = s // _TQ
    return pl.pallas_call(
        functools.partial(_nsa_kernel, s_len=s),
        grid=(bsz, nq),
        in_specs=[pl.BlockSpec((4 * LANES, _TQ), lambda bi, ci: (0, bi * nq + ci)),
                  pl.BlockSpec((1, s, 2 * LANES), lambda bi, ci: (bi, 0, 0)),
                  pl.BlockSpec((2 * LANES, s), lambda bi, ci: (0, bi)),
                  pl.BlockSpec((LANES, _TQ), lambda bi, ci: (0, bi * nq + ci)),
                  pl.BlockSpec((1, ncp, LANES), lambda bi, ci: (bi, 0, 0)),
                  pl.BlockSpec((1, LANES, ncp), lambda bi, ci: (bi, 0, 0)),
                  pl.BlockSpec(mt.shape, lambda bi, ci: (0, 0))],
        out_specs=pl.BlockSpec((1, _TQ, 4 * LANES), lambda bi, ci: (bi, ci, 0)),
        out_shape=jax.ShapeDtypeStruct((bsz, s, 4 * LANES), BF16),
        scratch_shapes=[pltpu.VMEM((n_sel, _TQ), F32), pltpu.VMEM((n_sel, _TQ), F32),
                        pltpu.VMEM((NSA_GROUPS, n_sel, _TQ), F32),
                        pltpu.VMEM((NSA_GROUPS, 1, cols), F32), pltpu.VMEM((NSA_GROUPS, LANES, cols), F32)],
        compiler_params=_cparams(("parallel", "arbitrary")),
        name="nsa_attention",
    )(qt, kk, vt, gt, kc, vct, mt)


_HC = 64


def _hgrn_consts():
    c = _HC
    t = np.arange(c)[:, None]
    u = np.arange(c)[None, :]
    sets = [u <= t,
            u > t,
            (u >= (t // 16) * 16) & (u <= t),
            (u >= (t // 4) * 4) & (u <= t)]
    for m in (1, 2, 3):
        sets.append((u > t) & (u <= 16 * m - 1))
    for m in (1, 2, 3):
        sets.append((u > t) & (u <= (t // 16) * 16 + 4 * m - 1))
    for m in (1, 2, 3):
        sets.append((u > t) & (u <= (t // 4) * 4 + m))
    dmat = np.concatenate([x.astype(np.float32) for x in sets], axis=0)

    r = np.arange(4 * c)[:, None] % c
    col = np.arange(4 * c)[None, :]
    slot, s = col // c, col % c
    m16 = (slot == r // 16 - 1) & (s // 16 < r // 16)
    m4 = (slot == (r % 16) // 4 - 1) & (s // 16 == r // 16) & ((s % 16) // 4 < (r % 16) // 4)
    m1 = (slot == r % 4) & (s // 4 == r // 4) & (s % 4 <= r % 4)
    masks = np.stack([m16, m4, m1]).astype(np.float32)
    hr = np.arange(4 * c)[:, None] // c
    hc = np.arange(4 * c)[None, :] // c
    bd = (hr == hc).astype(np.float32)
    return jnp.asarray(dmat, BF16), jnp.asarray(masks, F32), jnp.asarray(bd, F32)


def _hgrn_kernel(u_ref, lbl_ref, ng_ref, d_ref, mk_ref, bd_ref, o_ref, st_ref, *, layer, rb):
    w = HGRN_W
    c = _HC

    @pl.when(pl.program_id(1) == 0)
    def _():
        st_ref[...] = jnp.zeros((w, w), F32)

    lg = [lbl_ref[r:r + 1, :] for r in range(DEPTH)]
    mx = functools.reduce(jnp.maximum, lg)
    ex = [jnp.exp(x - mx) for x in lg]
    tot = functools.reduce(lambda a, b: a + b, ex)
    lb = jnp.zeros((1, w), F32)
    for r in range(1, layer + 1):
        lb = lb + ex[r] / tot

    bd = bd_ref[...]
    lane_head = lax.broadcasted_iota(jnp.int32, (c, w), 1) // HEAD_DIM
    head_keep = [lane_head == h for h in range(HGRN_HEADS)]
    head_keep_b = [jnp.where(m, 1.0, 0.0).astype(BF16) for m in head_keep]

    def stack_heads(x):
        return jnp.concatenate([head_keep_b[h] * x for h in range(HGRN_HEADS)], axis=0)

    def chunk(ci, carry):
        r0 = pl.multiple_of(ci * c, c)
        q = u_ref[0, pl.ds(r0, c), 0:w]
        fz = u_ref[0, pl.ds(r0, c), w:2 * w]
        v = u_ref[0, pl.ds(r0, c), 2 * w:3 * w]
        gz = u_ref[0, pl.ds(r0, c), 3 * w:4 * w]
        f = lb + (1.0 - lb) * _sigmoid(fz)
        logf = jnp.log(f)
        kk = 1.0 - f
        lhi = logf.astype(BF16)
        llo = (logf - lhi.astype(F32)).astype(BF16)
        ex_all = jnp.exp(_dot(d_ref[...], lhi) + _dot(d_ref[...], llo))

        def eset(i):
            return ex_all[i * c:(i + 1) * c, :]

        v_b = v.astype(BF16)
        kq = [q * eset(2), q * eset(3), q]
        zero_k = jnp.zeros_like(kk)
        kk_sets = [[kk * eset(4 + j) for j in range(3)] + [zero_k],
                   [kk * eset(7 + j) for j in range(3)] + [zero_k],
                   [kk] + [kk * eset(10 + j) for j in range(3)]]
        sw = jnp.zeros((HGRN_HEADS * c, HGRN_HEADS * c), F32)
        for lvl in range(3):
            lhs = stack_heads(kq[lvl].astype(BF16))
            rhs = jnp.concatenate([x.astype(BF16) for x in kk_sets[lvl]], axis=0)
            sw = sw + _dot_nt(lhs, rhs) * mk_ref[lvl]
        v_rep = jnp.concatenate([v_b] * HGRN_HEADS, axis=0)
        o_full = _dot(sw.astype(BF16), v_rep)
        o = jnp.zeros((c, w), F32)
        for h in range(HGRN_HEADS):
            o = o + jnp.where(head_keep[h], o_full[h * c:(h + 1) * c, :], 0.0)
        st = st_ref[...]
        o = o + _dot_nt((q * eset(0)).astype(BF16), st.astype(BF16))
        kst = (kk * eset(1)).astype(BF16)
        upd = lax.dot_general(v_b, kst, _TN, preferred_element_type=F32)
        st_ref[...] = ex_all[c - 1:c, :] * st + bd * upd
        ms = _split_dot(o * o, bd_ref[...].astype(BF16)) * (1.0 / HEAD_DIM)
        y = o * lax.rsqrt(ms + LN_EPS) * ng_ref[...]
        y = y * (gz * _sigmoid(gz))
        o_ref[0, pl.ds(r0, c), :] = y.astype(BF16)
        return carry

    lax.fori_loop(0, rb // c, chunk, 0, unroll=8)


def _hgrn(u, lb_logits, norm_g4, consts, layer, rb=512):
    bsz, s, _ = u.shape
    dmat, masks, bd = consts
    full = lambda arr: pl.BlockSpec(arr.shape, lambda bi, ci: (0,) * arr.ndim)
    return pl.pallas_call(
        functools.partial(_hgrn_kernel, layer=layer, rb=rb),
        grid=(bsz, s // rb),
        in_specs=[pl.BlockSpec((1, rb, 4 * HGRN_W), lambda bi, ci: (bi, ci, 0)),
                  full(lb_logits), full(norm_g4), full(dmat), full(masks), full(bd)],
        out_specs=pl.BlockSpec((1, rb, HGRN_W), lambda bi, ci: (bi, ci, 0)),
        out_shape=jax.ShapeDtypeStruct((bsz, s, HGRN_W), BF16),
        scratch_shapes=[pltpu.VMEM((HGRN_W, HGRN_W), F32)],
        compiler_params=_cparams(("parallel", "arbitrary")),
        name="hgrn2",
    )(u, lb_logits, norm_g4, dmat, masks, bd)


def _outproj_kernel(h_ref, yc_ref, yn_ref, yh_ref, wc_ref, wn_ref, wh_ref, g_ref, b_ref, o_ref):
    m = _dot(yc_ref[...], wc_ref[...]) + _dot(yn_ref[...], wn_ref[...]) + _dot(yh_ref[...], wh_ref[...])
    o_ref[...] = _layer_norm(DN_ALPHA * h_ref[...] + m, g_ref[...], b_ref[...])


def _outproj(h, yc, yn, yh, wc, wn, wh, g, b, tm=512):
    t = h.shape[0]
    row = lambda wd: pl.BlockSpec((tm, wd), lambda i: (i, 0))
    full = lambda arr: pl.BlockSpec(arr.shape, lambda i: (0,) * arr.ndim)
    return pl.pallas_call(
        _outproj_kernel,
        grid=(t // tm,),
        in_specs=[row(D_MODEL), row(CONV_CH), row(4 * LANES), row(HGRN_W),
                  full(wc), full(wn), full(wh), full(g), full(b)],
        out_specs=row(D_MODEL),
        out_shape=jax.ShapeDtypeStruct((t, D_MODEL), F32),
        compiler_params=_cparams(("parallel",)),
        name="outproj_ln1",
    )(h, yc, yn, yh, wc, wn, wh, g, b)


_FF_HALO = 8


def _ffn_kernel(h_ref, halo_ref, p_ref, wv_ref, wg_ref, cwv_ref, cwg_ref, cbv_ref, cbg_ref, wdn_ref, wpg_ref, wpp_ref,
                g_ref, b_ref, o_ref, acc_ref, *, tm, s_len):
    i = pl.program_id(0)
    j = pl.program_id(1)
    x = h_ref[...].astype(BF16)

    @pl.when(j == 0)
    def _():
        gate = _sigmoid(_dot(x, wpg_ref[...]))
        acc_ref[...] = gate * _dot(p_ref[...].astype(BF16), wpp_ref[...])

    seq_start = (i * tm) % s_len == 0
    halo = jnp.where(seq_start, 0.0, halo_ref[...]).astype(BF16)
    xh = jnp.concatenate([halo, x], axis=0)

    def conv_up(w_ref, cw_ref, cb_ref):
        up = _dot(xh, w_ref[...])
        cw = cw_ref[...]
        return (cw[0:1, :] * up[_FF_HALO - 2:_FF_HALO - 2 + tm] + cw[1:2, :] * up[_FF_HALO - 1:_FF_HALO - 1 + tm]
                + cw[2:3, :] * up[_FF_HALO:_FF_HALO + tm] + cb_ref[...])

    val = conv_up(wv_ref, cwv_ref, cbv_ref)
    gate = conv_up(wg_ref, cwg_ref, cbg_ref)
    act = (gate * _sigmoid(gate) * val).astype(BF16)
    acc_ref[...] += _dot(act, wdn_ref[...])

    @pl.when(j == pl.num_programs(1) - 1)
    def _():
        o_ref[...] = _layer_norm(DN_ALPHA * h_ref[...] + acc_ref[...], g_ref[...], b_ref[...])


def _ffn(h, p, wup, cw, cb, wdn, wpg, wpp, g, b, s_len, tm=512, nf=2):
    t = h.shape[0]
    tf = D_FF // nf
    hb = tm // _FF_HALO
    full = lambda arr: pl.BlockSpec(arr.shape, lambda i, j: (0,) * arr.ndim)
    half = lambda rows, off: pl.BlockSpec((rows, tf), lambda i, j: (0, off + j))
    return pl.pallas_call(
        functools.partial(_ffn_kernel, tm=tm, s_len=s_len),
        grid=(t // tm, nf),
        in_specs=[pl.BlockSpec((tm, D_MODEL), lambda i, j: (i, 0)),
                  pl.BlockSpec((_FF_HALO, D_MODEL), lambda i, j: (jnp.maximum(i * hb - 1, 0), 0)),
                  pl.BlockSpec((tm, D_PLE), lambda i, j: (i, 0)),
                  half(D_MODEL, 0), half(D_MODEL, nf), half(SUBLANES, 0), half(SUBLANES, nf), half(1, 0), half(1, nf),
                  pl.BlockSpec((tf, D_MODEL), lambda i, j: (j, 0)),
                  full(wpg), full(wpp), full(g), full(b)],
        out_specs=pl.BlockSpec((tm, D_MODEL), lambda i, j: (i, 0)),
        out_shape=jax.ShapeDtypeStruct((t, D_MODEL), F32),
        scratch_shapes=[pltpu.VMEM((tm, D_MODEL), F32)],
        compiler_params=_cparams(("parallel", "arbitrary")),
        name="ffn_ple_ln2",
    )(h, h, p, wup, wup, cw, cw, cb, cb, wdn, wpg, wpp, g, b)


def _pack_in_weights(w_in):
    kv0 = 1024
    col = lambda k: w_in[:, :, kv0 + k * LANES:kv0 + (k + 1) * LANES]
    w_nat = jnp.concatenate([w_in[:, :, 0:512], col(0), col(1), col(2), col(4), w_in[:, :, 1816:2840]], axis=-1)
    qw = w_in[:, :, 512:1024].reshape(DEPTH, D_MODEL, NSA_GROUPS, NSA_HPG, HEAD_DIM)
    qw = qw.transpose(0, 1, 3, 2, 4).reshape(DEPTH, D_MODEL, 512)
    gw = jnp.pad(w_in[:, :, 1792:1816], ((0, 0), (0, 0), (0, LANES - 24)))
    w_t = jnp.concatenate([qw, col(3), col(5), gw], axis=-1).transpose(0, 2, 1)
    return w_nat.astype(BF16), w_t.astype(BF16)


def _pack_cmp_weights(pe, w1, w2):
    d = pe.shape[0]
    eye = jnp.eye(NSA_GROUPS, dtype=F32)
    pe2 = jnp.broadcast_to(pe.reshape(d, 2, 16, 1, HEAD_DIM), (d, 2, 16, NSA_GROUPS, HEAD_DIM)).reshape(d, 2, 2048)
    w1r = w1.reshape(d, 2, 16, HEAD_DIM, NSA_CMP_HIDDEN)
    w1e = jnp.einsum('dplkh,ge->dplgkeh', w1r, eye).reshape(d, 2, 2048, NSA_GROUPS * NSA_CMP_HIDDEN)
    w2e = jnp.einsum('dhk,ge->dghek', w2, eye).reshape(d, NSA_GROUPS * NSA_CMP_HIDDEN, LANES)
    return pe2, w1e[:, 0].astype(BF16), w1e[:, 1].astype(BF16), w2e.astype(BF16), w2e.transpose(0, 2, 1).astype(BF16)


def _pack_out_weights(w_out):
    wc = w_out[:, 0:CONV_CH]
    wn = w_out[:, CONV_CH:CONV_CH + 512].reshape(DEPTH, NSA_GROUPS, NSA_HPG, HEAD_DIM, D_MODEL)
    wn = wn.transpose(0, 2, 1, 3, 4).reshape(DEPTH, 512, D_MODEL)
    wh = w_out[:, CONV_CH + 512:]
    return wc.astype(BF16), wn.astype(BF16), wh.astype(BF16)


def _pack_ffn_weights(w_up, conv_w, conv_b, w_down):
    cw = jnp.pad(conv_w, ((0, 0), (0, SUBLANES - conv_w.shape[1]), (0, 0)))
    return w_up.astype(BF16), cw, conv_b[:, None, :], w_down.astype(BF16)


def kernel(x, p, w_in, conv_w, conv_b, conv_ln_g, conv_ln_b, cmp_pe_k, cmp_pe_v, cmp_w1_k, cmp_w2_k, cmp_w1_v,
           cmp_w2_v, lb_logits, hgrn_norm_g, w_out, ln1_g, ln1_b, w_up, ffn_conv_w, ffn_conv_b, w_down,
           w_ple_gate, w_ple_proj, ln2_g, ln2_b):
    bsz, s, _ = x.shape
    t = bsz * s
    w_nat, w_t = _pack_in_weights(w_in)
    pe_k, w1k_t, w1k_b, w2k, _ = _pack_cmp_weights(cmp_pe_k, cmp_w1_k, cmp_w2_k)
    pe_v, w1v_t, w1v_b, _, w2v_t = _pack_cmp_weights(cmp_pe_v, cmp_w1_v, cmp_w2_v)
    wc, wn, wh = _pack_out_weights(w_out)
    wup, fcw, fcb, wdn = _pack_ffn_weights(w_up, ffn_conv_w, ffn_conv_b, w_down)
    conv_w_p = jnp.pad(conv_w, ((0, 0), (0, 1), (0, 0)))
    norm_g4 = jnp.tile(hgrn_norm_g, (1, HGRN_HEADS))[:, None, :]
    wpg = w_ple_gate.astype(BF16)
    wpp = w_ple_proj.astype(BF16)
    mt = _nsa_consts(s)
    hgrn_consts = _hgrn_consts()
    n_rows = s // NSA_CMP_STRIDE

    h = x.reshape(t, D_MODEL)
    for i in range(DEPTH):
        conv_u, kcu, vcu, kk, hg, qt, vt, gt = _inproj(h, w_nat[i], w_t[i])
        y_conv = _convmod(conv_u.reshape(bsz, s, 2 * CONV_CH), conv_w_p[i], conv_b[i][None], conv_ln_g[i][None],
                          conv_ln_b[i][None])
        kc = _compress(kcu.reshape(bsz, n_rows, NSA_CMP_STRIDE * LANES), pe_k[i], w1k_t[i], w1k_b[i], w2k[i], False)
        vct = _compress(vcu.reshape(bsz, n_rows, NSA_CMP_STRIDE * LANES), pe_v[i], w1v_t[i], w1v_b[i], w2v_t[i], True)
        y_nsa = _nsa_attention(qt, kk.reshape(bsz, s, 2 * LANES), vt, gt, kc, vct, mt, bsz, s)
        y_hgrn = _hgrn(hg.reshape(bsz, s, 4 * HGRN_W), lb_logits, norm_g4[i], hgrn_consts, i)
        h1 = _outproj(h, y_conv.reshape(t, CONV_CH), y_nsa.reshape(t, 512), y_hgrn.reshape(t, HGRN_W),
                      wc[i], wn[i], wh[i], ln1_g[i][None], ln1_b[i][None])
        h = _ffn(h1, p[i].reshape(t, D_PLE), wup[i], fcw[i], fcb[i], wdn[i], wpg[i], wpp[i],
                 ln2_g[i][None], ln2_b[i][None], s)
    return h.reshape(bsz, s, D_MODEL)
```

```python
import functools

import numpy as np
import jax
import jax.numpy as jnp
from jax import lax
from jax.experimental import pallas as pl
from jax.experimental.pallas import tpu as pltpu

F32 = jnp.float32
BF16 = jnp.bfloat16

D_MODEL = 1024
DEPTH = 4
D_PLE = 256
HEAD_DIM = 64
CONV_CH = 256
CONV_K = 31
NSA_HEADS = 8
NSA_GROUPS = 2
NSA_HPG = NSA_HEADS // NSA_GROUPS
NSA_CMP_STRIDE = 16
NSA_CMP_LEN = 32
NSA_CMP_HIDDEN = 128
NSA_SEL_LEN = 64
NSA_TOP = 16
NSA_WINDOW = 512
HGRN_HEADS = 4
HGRN_W = HGRN_HEADS * HEAD_DIM
D_FF = 2816
DN_ALPHA = (2 * DEPTH) ** 0.25
ATTN_SCALE = HEAD_DIM ** -0.5
LOG2_E = 1.4426950408889634
LN_EPS = 1e-5
MASK_VALUE = -1e30
FORCE_SCORE = 1e9

LANES = 128
SUBLANES = 8
VMEM_LIMIT = 56 * 1024 * 1024

_O_CONV, _O_KC, _O_VC, _O_KK, _O_HG, _O_END = 0, 512, 640, 768, 1024, 2048
_R_Q, _R_V, _R_GATE, _R_END = 0, 512, 768, 896

_NT = (((1,), (1,)), ((), ()))
_TN = (((0,), (0,)), ((), ()))


def _cparams(sem):
    return pltpu.CompilerParams(dimension_semantics=sem, vmem_limit_bytes=VMEM_LIMIT)


def _dot(a, b):
    return jnp.dot(a, b, preferred_element_type=F32)


def _dot_nt(a, b):
    return lax.dot_general(a, b, _NT, preferred_element_type=F32)


def _split_dot(x, w):
    hi = x.astype(BF16)
    lo = (x - hi.astype(F32)).astype(BF16)
    return _dot(hi, w) + _dot(lo, w)


def _sigmoid(x):
    return 1.0 / (1.0 + jnp.exp(-x))


def _layer_norm(z, g, b):
    mu = jnp.mean(z, axis=-1, keepdims=True)
    zc = z - mu
    var = jnp.mean(zc * zc, axis=-1, keepdims=True)
    return zc * lax.rsqrt(var + LN_EPS) * g + b


def _inproj_kernel(h_ref, w_ref, wt_ref, conv_ref, kc_ref, vc_ref, kk_ref, hg_ref, qt_ref, vt_ref, gt_ref):
    x = h_ref[...].astype(BF16)
    conv_ref[...] = _dot(x, w_ref[:, _O_CONV:_O_KC])
    kc_ref[...] = _dot(x, w_ref[:, _O_KC:_O_VC]).astype(BF16)
    vc_ref[...] = _dot(x, w_ref[:, _O_VC:_O_KK]).astype(BF16)
    kk_ref[...] = _dot(x, w_ref[:, _O_KK:_O_HG]).astype(BF16)
    hg_ref[...] = _dot(x, w_ref[:, _O_HG:_O_END])
    qt_ref[...] = _dot_nt(wt_ref[_R_Q:_R_V, :], x).astype(BF16)
    vt_ref[...] = _dot_nt(wt_ref[_R_V:_R_GATE, :], x).astype(BF16)
    gt_ref[...] = _dot_nt(wt_ref[_R_GATE:_R_END, :], x)


def _inproj(h, w, wt, tm=512):
    t = h.shape[0]
    widths = (512, 128, 128, 256, 1024)
    dtypes = (F32, BF16, BF16, BF16, F32)
    heights = (512, 256, 128)
    hdtypes = (BF16, BF16, F32)
    return pl.pallas_call(
        _inproj_kernel,
        grid=(t // tm,),
        in_specs=[pl.BlockSpec((tm, D_MODEL), lambda i: (i, 0)),
                  pl.BlockSpec((D_MODEL, _O_END), lambda i: (0, 0)),
                  pl.BlockSpec((_R_END, D_MODEL), lambda i: (0, 0))],
        out_specs=([pl.BlockSpec((tm, wd), lambda i: (i, 0)) for wd in widths]
                   + [pl.BlockSpec((ht, tm), lambda i: (0, i)) for ht in heights]),
        out_shape=([jax.ShapeDtypeStruct((t, wd), dt) for wd, dt in zip(widths, dtypes)]
                   + [jax.ShapeDtypeStruct((ht, t), dt) for ht, dt in zip(heights, hdtypes)]),
        compiler_params=_cparams(("parallel",)),
        name="inproj",
    )(h, w, wt)


_CONV_HALO = 32
_CONV_SUB = 64


def _convmod_kernel(u_ref, w_ref, b_ref, g_ref, beta_ref, o_ref, buf_ref, *, tt):
    @pl.when(pl.program_id(1) == 0)
    def _():
        buf_ref[0:_CONV_HALO, :] = jnp.zeros((_CONV_HALO, CONV_CH), F32)

    u = u_ref[0]
    buf_ref[_CONV_HALO:_CONV_HALO + tt, :] = u[:, :CONV_CH] * _sigmoid(u[:, CONV_CH:])
    w = w_ref[...]
    base = _CONV_HALO - (CONV_K - 1)
    for r in range(tt // _CONV_SUB):
        acc = jnp.broadcast_to(b_ref[...], (_CONV_SUB, CONV_CH))
        for k in range(CONV_K):
            s0 = r * _CONV_SUB + base + k
            acc = acc + w[k:k + 1, :] * buf_ref[s0:s0 + _CONV_SUB, :]
        y = _layer_norm(acc, g_ref[...], beta_ref[...])
        o_ref[0, r * _CONV_SUB:(r + 1) * _CONV_SUB, :] = (y * _sigmoid(y)).astype(BF16)
    buf_ref[0:_CONV_HALO, :] = buf_ref[tt:tt + _CONV_HALO, :]


def _convmod(u, w, b, g, beta, tt=512):
    bsz, s, _ = u.shape
    vec = pl.BlockSpec((1, CONV_CH), lambda bi, ti: (0, 0))
    return pl.pallas_call(
        functools.partial(_convmod_kernel, tt=tt),
        grid=(bsz, s // tt),
        in_specs=[pl.BlockSpec((1, tt, 2 * CONV_CH), lambda bi, ti: (bi, ti, 0)),
                  pl.BlockSpec((CONV_K + 1, CONV_CH), lambda bi, ti: (0, 0)), vec, vec, vec],
        out_specs=pl.BlockSpec((1, tt, CONV_CH), lambda bi, ti: (bi, ti, 0)),
        out_shape=jax.ShapeDtypeStruct((bsz, s, CONV_CH), BF16),
        scratch_shapes=[pltpu.VMEM((tt + _CONV_HALO, CONV_CH), F32)],
        compiler_params=_cparams(("parallel", "arbitrary")),
        name="convmod",
    )(u, w, b, g, beta)


def _gelu_tanh(x):
    return 0.5 * x * (1.0 + jnp.tanh(0.7978845608028654 * (x + 0.044715 * (x * x * x))))


def _compress_kernel(x_ref, pe_ref, wt_ref, wb_ref, w2_ref, o_ref, *, feature_major):
    x = x_ref[0].astype(F32)
    n = x.shape[0]
    top = _dot((x + pe_ref[0:1, :]).astype(BF16), wt_ref[...])
    bot = _dot((x + pe_ref[1:2, :]).astype(BF16), wb_ref[...])
    hid = top + pltpu.roll(bot, n - 1, 0)
    act = _gelu_tanh(hid).astype(BF16)
    if feature_major:
        o_ref[0] = _dot_nt(w2_ref[...], act).astype(BF16)
    else:
        o_ref[0] = _dot(act, w2_ref[...]).astype(BF16)


def _compress(xr, pe2, wt, wb, w2, feature_major):
    bsz, n, width = xr.shape
    hid = wt.shape[1]
    full = lambda shape: pl.BlockSpec(shape, lambda bi: (0,) * len(shape))
    oshape = (LANES, n) if feature_major else (n, LANES)
    return pl.pallas_call(
        functools.partial(_compress_kernel, feature_major=feature_major),
        grid=(bsz,),
        in_specs=[pl.BlockSpec((1, n, width), lambda bi: (bi, 0, 0)),
                  full((2, width)), full((width, hid)), full((width, hid)), full(w2.shape)],
        out_specs=pl.BlockSpec((1,) + oshape, lambda bi: (bi, 0, 0)),
        out_shape=jax.ShapeDtypeStruct((bsz,) + oshape, BF16),
        compiler_params=_cparams(("parallel",)),
        name="nsa_compress",
    )(xr, pe2, wt, wb, w2)


_TQ = 128
_TK = 512


def _nsa_kernel(qt_ref, kk_ref, vt_ref, gt_ref, kc_ref, vct_ref, mt_ref, o_ref,
                sc_ref, cnt_ref, selb_ref, m_ref, acc_ref, *, s_len):
    tq = _TQ
    cols = NSA_HPG * tq
    n_cmp = s_len // NSA_CMP_STRIDE - 1
    n_sel = s_len // NSA_SEL_LEN
    n_grp = n_sel // SUBLANES
    t0 = pl.program_id(1) * tq
    qt = qt_ref[...]
    row_grp = lax.broadcasted_iota(jnp.int32, (LANES, tq), 0) // HEAD_DIM
    keep = [jnp.where(row_grp == g, 1.0, 0.0).astype(BF16) for g in range(NSA_GROUPS)]

    def stack_q(g):
        return jnp.concatenate([keep[g] * qt[hh * LANES:(hh + 1) * LANES, :] for hh in range(NSA_HPG)], axis=1)

    qg = [stack_q(g) for g in range(NSA_GROUPS)]

    def per_head(s, fn):
        return jnp.concatenate([fn(s[:, hh * tq:(hh + 1) * tq]) for hh in range(NSA_HPG)], axis=1)

    def own_rows(v_t, g):
        kb = jnp.where(lax.broadcasted_iota(jnp.int32, v_t.shape, 0) // HEAD_DIM == g, 1.0, 0.0).astype(BF16)
        return kb * v_t + (1.0 - kb)

    def normalise(acc, g):
        oth = (1 - g) * HEAD_DIM
        return acc * (1.0 / acc[oth:oth + 1, :])

    kc = kc_ref[0]
    vct = vct_ref[0]
    ncp = kc.shape[0]
    n_i = lax.broadcasted_iota(jnp.int32, (ncp, tq), 0)
    pos1 = t0 + lax.broadcasted_iota(jnp.int32, (ncp, tq), 1)
    ok1 = (n_i * NSA_CMP_STRIDE + (NSA_CMP_LEN - 1) <= pos1) & (n_i < n_cmp)
    bias1 = jnp.where(ok1, 0.0, MASK_VALUE)
    live1 = jnp.where(ok1, 1.0, 0.0)
    jblk = lax.broadcasted_iota(jnp.int32, (n_sel, tq), 0)
    blk = (t0 + lax.broadcasted_iota(jnp.int32, (n_sel, tq), 1)) >> 6
    causal = jblk <= blk
    forced = (jblk == 0) | (jblk == blk) | (jblk == blk - 1)
    n_live_blocks = (t0 + tq) // NSA_SEL_LEN
    jrow = lax.broadcasted_iota(jnp.int32, (SUBLANES, tq), 0)
    o_cmp = []
    raw1 = [_dot(kc, qg[g]) for g in range(NSA_GROUPS)]
    for g in range(NSA_GROUPS):
        s1 = per_head(raw1[g], lambda x: x + bias1)
        mx = jnp.max(s1, axis=0, keepdims=True)
        p1 = per_head(jnp.exp2(s1 - mx), lambda x: x * live1)
        den = jnp.sum(p1, axis=0, keepdims=True)
        p1 = p1 * (1.0 / jnp.where(den > 0.0, den, 1.0))
        o_cmp.append(_dot(vct, p1.astype(BF16)))
        psum = p1[:, 0:tq] + p1[:, tq:2 * tq] + p1[:, 2 * tq:3 * tq] + p1[:, 3 * tq:4 * tq]
        p_hi = psum.astype(BF16)
        p_lo = (psum - p_hi.astype(F32)).astype(BF16)
        imp = _dot(mt_ref[...], p_hi) + _dot(mt_ref[...], p_lo)
        sc_ref[...] = jnp.where(causal, jnp.where(forced, FORCE_SCORE, imp), -jnp.inf)
        cnt_ref[...] = jnp.zeros((n_sel, tq), F32)
        for kb in range(n_grp):
            @pl.when((kb * SUBLANES < n_live_blocks) & (n_live_blocks > NSA_TOP))
            def _(kb=kb):
                sc = sc_ref[...]
                grp = [sc[j * SUBLANES:(j + 1) * SUBLANES, :] for j in range(n_grp)]
                cnt = [cnt_ref[j * SUBLANES:(j + 1) * SUBLANES, :] for j in range(n_grp)]
                for k in range(kb * SUBLANES, (kb + 1) * SUBLANES):
                    rk = sc[k:k + 1, :]
                    for j in range(n_grp):
                        if j > kb:
                            inc = jnp.where(rk >= grp[j], 1.0, 0.0)
                        elif j < kb:
                            inc = jnp.where(rk > grp[j], 1.0, 0.0)
                        else:
                            tie = jnp.where(jrow + j * SUBLANES > k, 1.0, 0.0)
                            inc = jnp.where(rk > grp[j], 1.0, jnp.where(rk == grp[j], tie, 0.0))
                        cnt[j] = cnt[j] + inc
                for j in range(n_grp):
                    cnt_ref[j * SUBLANES:(j + 1) * SUBLANES, :] = cnt[j]
        selb_ref[g] = jnp.where((cnt_ref[...] < NSA_TOP) & causal, 0.0, MASK_VALUE)

    n_chunks = (t0 + tq + _TK - 1) // _TK
    blocks_per_chunk = _TK // NSA_SEL_LEN
    krow = lax.broadcasted_iota(jnp.int32, (_TK, tq), 0)
    posq = t0 + lax.broadcasted_iota(jnp.int32, (_TK, tq), 1)
    m_ref[...] = jnp.full((NSA_GROUPS, 1, cols), MASK_VALUE, F32)
    acc_ref[...] = jnp.zeros((NSA_GROUPS, LANES, cols), F32)

    def chunk(i, carry):
        k0 = pl.multiple_of(i * _TK, _TK)
        ks = kk_ref[0, pl.ds(k0, _TK), 0:LANES]
        vs = vt_ref[0:LANES, pl.ds(k0, _TK)]
        causal2 = krow + k0 <= posq
        raw = [_dot(ks, qg[g]) for g in range(NSA_GROUPS)]
        for g in range(NSA_GROUPS):
            rows = [jnp.broadcast_to(selb_ref[g, pl.ds(i * blocks_per_chunk + j, 1), :], (NSA_SEL_LEN, tq))
                    for j in range(blocks_per_chunk)]
            bias = jnp.where(causal2, jnp.concatenate(rows, axis=0), MASK_VALUE)
            s2 = per_head(raw[g], lambda x: x + bias)
            m_old = m_ref[g]
            m_new = jnp.maximum(m_old, jnp.max(s2, axis=0, keepdims=True))
            p = jnp.exp2(s2 - m_new).astype(BF16)
            acc_ref[g] = jnp.exp2(m_old - m_new) * acc_ref[g] + _dot(own_rows(vs, g), p)
            m_ref[g] = m_new
        return carry

    lax.fori_loop(0, n_chunks, chunk, 0)
    o_slc = [normalise(acc_ref[g], g) for g in range(NSA_GROUPS)]

    wk = tq + NSA_WINDOW
    w0 = pl.multiple_of(jnp.maximum(t0 - NSA_WINDOW, 0), tq)
    kw = kk_ref[0, pl.ds(w0, wk), LANES:2 * LANES]
    vw_t = vt_ref[LANES:2 * LANES, pl.ds(w0, wk)]
    kpos = w0 + lax.broadcasted_iota(jnp.int32, (wk, tq), 0)
    pos3 = t0 + lax.broadcasted_iota(jnp.int32, (wk, tq), 1)
    bias3 = jnp.where((kpos <= pos3) & (pos3 - kpos < NSA_WINDOW), 0.0, MASK_VALUE)
    o_win = []
    raw3 = [_dot(kw, qg[g]) for g in range(NSA_GROUPS)]
    for g in range(NSA_GROUPS):
        s3 = per_head(raw3[g], lambda x: x + bias3)
        p3 = jnp.exp2(s3 - jnp.max(s3, axis=0, keepdims=True)).astype(BF16)
        o_win.append(normalise(_dot(own_rows(vw_t, g), p3), g))

    gates = _sigmoid(gt_ref[...])
    first = row_grp == 0
    for hh in range(NSA_HPG):
        y = jnp.zeros((LANES, tq), F32)
        for br, o in enumerate((o_cmp, o_slc, o_win)):
            r_a, r_b = hh * 3 + br, (NSA_HPG + hh) * 3 + br
            y = y + jnp.where(first, gates[r_a:r_a + 1, :] * o[0][:, hh * tq:(hh + 1) * tq],
                              gates[r_b:r_b + 1, :] * o[1][:, hh * tq:(hh + 1) * tq])
        o_ref[0, :, hh * LANES:(hh + 1) * LANES] = y.T.astype(BF16)


def _nsa_consts(s_len):
    n_cmp_pad = s_len // NSA_CMP_STRIDE
    n_cmp = n_cmp_pad - 1
    n_sel = s_len // NSA_SEL_LEN
    cmp_start = np.arange(n_cmp) * NSA_CMP_STRIDE
    cmp_end = cmp_start + NSA_CMP_LEN - 1
    sel_start = np.arange(n_sel) * NSA_SEL_LEN
    m = ((cmp_start[:, None] <= sel_start[None, :] + NSA_SEL_LEN - 1) & (cmp_end[:, None] >= sel_start[None, :]))
    mt = np.zeros((n_sel, n_cmp_pad), np.float32)
    mt[:, :n_cmp] = m.T
    return jnp.asarray(mt, BF16)


def _nsa_attention(qt, kk, vt, gt, kc, vct, mt, bsz, s):
    assert s % _TK == 0 and s >= _TQ + NSA_WINDOW and (s // NSA_SEL_LEN) % SUBLANES == 0
    ncp = kc.shape[1]
    n_sel = s // NSA_SEL_LEN
    cols = NSA_HPG * _TQ
    nq = s // _TQ
    return pl.pallas_call(
        functools.partial(_nsa_kernel, s_len=s),
        grid=(bsz, nq),
        in_specs=[pl.BlockSpec((4 * LANES, _TQ), lambda bi, ci: (0, bi * nq + ci)),
                  pl.BlockSpec((1, s, 2 * LANES), lambda bi, ci: (bi, 0, 0)),
                  pl.BlockSpec((2 * LANES, s), lambda bi, ci: (0, bi)),
                  pl.BlockSpec((LANES, _TQ), lambda bi, ci: (0, bi * nq + ci)),
                  pl.BlockSpec((1, ncp, LANES), lambda bi, ci: (bi, 0, 0)),
                  pl.BlockSpec((1, LANES, ncp), lambda bi, ci: (bi, 0, 0)),
                  pl.BlockSpec(mt.shape, lambda bi, ci: (0, 0))],
        out_specs=pl.BlockSpec((1, _TQ, 4 * LANES), lambda bi, ci: (bi, ci, 0)),
        out_shape=jax.ShapeDtypeStruct((bsz, s, 4 * LANES), BF16),
        scratch_shapes=[pltpu.VMEM((n_sel, _TQ), F32), pltpu.VMEM((n_sel, _TQ), F32),
                        pltpu.VMEM((NSA_GROUPS, n_sel, _TQ), F32),
                        pltpu.VMEM((NSA_GROUPS, 1, cols), F32), pltpu.VMEM((NSA_GROUPS, LANES, cols), F32)],
        compiler_params=_cparams(("parallel", "arbitrary")),
        name="nsa_attention",
    )(qt, kk, vt, gt, kc, vct, mt)


_HC = 64


def _hgrn_consts():
    c = _HC
    t = np.arange(c)[:, None]
    u = np.arange(c)[None, :]
    sets = [u <= t,
            u > t,
            (u >= (t // 16) * 16) & (u <= t),
            (u >= (t // 4) * 4) & (u <= t)]
    for m in (1, 2, 3):
        sets.append((u > t) & (u <= 16 * m - 1))
    for m in (1, 2, 3):
        sets.append((u > t) & (u <= (t // 16) * 16 + 4 * m - 1))
    for m in (1, 2, 3):
        sets.append((u > t) & (u <= (t // 4) * 4 + m))
    dmat = np.concatenate([x.astype(np.float32) for x in sets], axis=0)

    r = np.arange(4 * c)[:, None] % c
    col = np.arange(4 * c)[None, :]
    slot, s = col // c, col % c
    m16 = (slot == r // 16 - 1) & (s // 16 < r // 16)
    m4 = (slot == (r % 16) // 4 - 1) & (s // 16 == r // 16) & ((s % 16) // 4 < (r % 16) // 4)
    m1 = (slot == r % 4) & (s // 4 == r // 4) & (s % 4 <= r % 4)
    masks = np.stack([m16, m4, m1]).astype(np.float32)
    hr = np.arange(4 * c)[:, None] // c
    hc = np.arange(4 * c)[None, :] // c
    bd = (hr == hc).astype(np.float32)
    return jnp.asarray(dmat, BF16), jnp.asarray(masks, F32), jnp.asarray(bd, F32)


def _hgrn_kernel(u_ref, lbl_ref, ng_ref, d_ref, mk_ref, bd_ref, o_ref, st_ref, *, layer, rb):
    w = HGRN_W
    c = _HC

    @pl.when(pl.program_id(1) == 0)
    def _():
        st_ref[...] = jnp.zeros((w, w), F32)

    lg = [lbl_ref[r:r + 1, :] for r in range(DEPTH)]
    mx = functools.reduce(jnp.maximum, lg)
    ex = [jnp.exp(x - mx) for x in lg]
    tot = functools.reduce(lambda a, b: a + b, ex)
    lb = jnp.zeros((1, w), F32)
    for r in range(1, layer + 1):
        lb = lb + ex[r] / tot

    bd = bd_ref[...]
    lane_head = lax.broadcasted_iota(jnp.int32, (c, w), 1) // HEAD_DIM
    head_keep = [lane_head == h for h in range(HGRN_HEADS)]
    head_keep_b = [jnp.where(m, 1.0, 0.0).astype(BF16) for m in head_keep]

    def stack_heads(x):
        return jnp.concatenate([head_keep_b[h] * x for h in range(HGRN_HEADS)], axis=0)

    def chunk(ci, carry):
        r0 = pl.multiple_of(ci * c, c)
        q = u_ref[0, pl.ds(r0, c), 0:w]
        fz = u_ref[0, pl.ds(r0, c), w:2 * w]
        v = u_ref[0, pl.ds(r0, c), 2 * w:3 * w]
        gz = u_ref[0, pl.ds(r0, c), 3 * w:4 * w]
        f = lb + (1.0 - lb) * _sigmoid(fz)
        logf = jnp.log(f)
        kk = 1.0 - f
        lhi = logf.astype(BF16)
        llo = (logf - lhi.astype(F32)).astype(BF16)
        ex_all = jnp.exp(_dot(d_ref[...], lhi) + _dot(d_ref[...], llo))

        def eset(i):
            return ex_all[i * c:(i + 1) * c, :]

        v_b = v.astype(BF16)
        kq = [q * eset(2), q * eset(3), q]
        zero_k = jnp.zeros_like(kk)
        kk_sets = [[kk * eset(4 + j) for j in range(3)] + [zero_k],
                   [kk * eset(7 + j) for j in range(3)] + [zero_k],
                   [kk] + [kk * eset(10 + j) for j in range(3)]]
        sw = jnp.zeros((HGRN_HEADS * c, HGRN_HEADS * c), F32)
        for lvl in range(3):
            lhs = stack_heads(kq[lvl].astype(BF16))
            rhs = jnp.concatenate([x.astype(BF16) for x in kk_sets[lvl]], axis=0)
            sw = sw + _dot_nt(lhs, rhs) * mk_ref[lvl]
        v_rep = jnp.concatenate([v_b] * HGRN_HEADS, axis=0)
        o_full = _dot(sw.astype(BF16), v_rep)
        o = jnp.zeros((c, w), F32)
        for h in range(HGRN_HEADS):
            o = o + jnp.where(head_keep[h], o_full[h * c:(h + 1) * c, :], 0.0)
        st = st_ref[...]
        o = o + _dot_nt((q * eset(0)).astype(BF16), st.astype(BF16))
        kst = (kk * eset(1)).astype(BF16)
        upd = lax.dot_general(v_b, kst, _TN, preferred_element_type=F32)
        st_ref[...] = ex_all[c - 1:c, :] * st + bd * upd
        ms = _split_dot(o * o, bd_ref[...].astype(BF16)) * (1.0 / HEAD_DIM)
        y = o * lax.rsqrt(ms + LN_EPS) * ng_ref[...]
        y = y * (gz * _sigmoid(gz))
        o_ref[0, pl.ds(r0, c), :] = y.astype(BF16)
        return carry

    lax.fori_loop(0, rb // c, chunk, 0, unroll=8)


def _hgrn(u, lb_logits, norm_g4, consts, layer, rb=512):
    bsz, s, _ = u.shape
    dmat, masks, bd = consts
    full = lambda arr: pl.BlockSpec(arr.shape, lambda bi, ci: (0,) * arr.ndim)
    return pl.pallas_call(
        functools.partial(_hgrn_kernel, layer=layer, rb=rb),
        grid=(bsz, s // rb),
        in_specs=[pl.BlockSpec((1, rb, 4 * HGRN_W), lambda bi, ci: (bi, ci, 0)),
                  full(lb_logits), full(norm_g4), full(dmat), full(masks), full(bd)],
        out_specs=pl.BlockSpec((1, rb, HGRN_W), lambda bi, ci: (bi, ci, 0)),
        out_shape=jax.ShapeDtypeStruct((bsz, s, HGRN_W), BF16),
        scratch_shapes=[pltpu.VMEM((HGRN_W, HGRN_W), F32)],
        compiler_params=_cparams(("parallel", "arbitrary")),
        name="hgrn2",
    )(u, lb_logits, norm_g4, dmat, masks, bd)


def _outproj_kernel(h_ref, yc_ref, yn_ref, yh_ref, wc_ref, wn_ref, wh_ref, g_ref, b_ref, o_ref):
    m = _dot(yc_ref[...], wc_ref[...]) + _dot(yn_ref[...], wn_ref[...]) + _dot(yh_ref[...], wh_ref[...])
    o_ref[...] = _layer_norm(DN_ALPHA * h_ref[...] + m, g_ref[...], b_ref[...])


def _outproj(h, yc, yn, yh, wc, wn, wh, g, b, tm=512):
    t = h.shape[0]
    row = lambda wd: pl.BlockSpec((tm, wd), lambda i: (i, 0))
    full = lambda arr: pl.BlockSpec(arr.shape, lambda i: (0,) * arr.ndim)
    return pl.pallas_call(
        _outproj_kernel,
        grid=(t // tm,),
        in_specs=[row(D_MODEL), row(CONV_CH), row(4 * LANES), row(HGRN_W),
                  full(wc), full(wn), full(wh), full(g), full(b)],
        out_specs=row(D_MODEL),
        out_shape=jax.ShapeDtypeStruct((t, D_MODEL), F32),
        compiler_params=_cparams(("parallel",)),
        name="outproj_ln1",
    )(h, yc, yn, yh, wc, wn, wh, g, b)


_FF_HALO = 8


def _ffn_kernel(h_ref, halo_ref, p_ref, wv_ref, wg_ref, cwv_ref, cwg_ref, cbv_ref, cbg_ref, wdn_ref, wpg_ref, wpp_ref,
                g_ref, b_ref, o_ref, acc_ref, *, tm, s_len):
    i = pl.program_id(0)
    j = pl.program_id(1)
    x = h_ref[...].astype(BF16)

    @pl.when(j == 0)
    def _():
        gate = _sigmoid(_dot(x, wpg_ref[...]))
        acc_ref[...] = gate * _dot(p_ref[...].astype(BF16), wpp_ref[...])

    seq_start = (i * tm) % s_len == 0
    halo = jnp.where(seq_start, 0.0, halo_ref[...]).astype(BF16)
    xh = jnp.concatenate([halo, x], axis=0)

    def conv_up(w_ref, cw_ref, cb_ref):
        up = _dot(xh, w_ref[...])
        cw = cw_ref[...]
        return (cw[0:1, :] * up[_FF_HALO - 2:_FF_HALO - 2 + tm] + cw[1:2, :] * up[_FF_HALO - 1:_FF_HALO - 1 + tm]
                + cw[2:3, :] * up[_FF_HALO:_FF_HALO + tm] + cb_ref[...])

    val = conv_up(wv_ref, cwv_ref, cbv_ref)
    gate = conv_up(wg_ref, cwg_ref, cbg_ref)
    act = (gate * _sigmoid(gate) * val).astype(BF16)
    acc_ref[...] += _dot(act, wdn_ref[...])

    @pl.when(j == pl.num_programs(1) - 1)
    def _():
        o_ref[...] = _layer_norm(DN_ALPHA * h_ref[...] + acc_ref[...], g_ref[...], b_ref[...])


def _ffn(h, p, wup, cw, cb, wdn, wpg, wpp, g, b, s_len, tm=512, nf=2):
    t = h.shape[0]
    tf = D_FF // nf
    hb = tm // _FF_HALO
    full = lambda arr: pl.BlockSpec(arr.shape, lambda i, j: (0,) * arr.ndim)
    half = lambda rows, off: pl.BlockSpec((rows, tf), lambda i, j: (0, off + j))
    return pl.pallas_call(
        functools.partial(_ffn_kernel, tm=tm, s_len=s_len),
        grid=(t // tm, nf),
        in_specs=[pl.BlockSpec((tm, D_MODEL), lambda i, j: (i, 0)),
                  pl.BlockSpec((_FF_HALO, D_MODEL), lambda i, j: (jnp.maximum(i * hb - 1, 0), 0)),
                  pl.BlockSpec((tm, D_PLE), lambda i, j: (i, 0)),
                  half(D_MODEL, 0), half(D_MODEL, nf), half(SUBLANES, 0), half(SUBLANES, nf), half(1, 0), half(1, nf),
                  pl.BlockSpec((tf, D_MODEL), lambda i, j: (j, 0)),
                  full(wpg), full(wpp), full(g), full(b)],
        out_specs=pl.BlockSpec((tm, D_MODEL), lambda i, j: (i, 0)),
        out_shape=jax.ShapeDtypeStruct((t, D_MODEL), F32),
        scratch_shapes=[pltpu.VMEM((tm, D_MODEL), F32)],
        compiler_params=_cparams(("parallel", "arbitrary")),
        name="ffn_ple_ln2",
    )(h, h, p, wup, wup, cw, cw, cb, cb, wdn, wpg, wpp, g, b)


def _pack_in_weights(w_in):
    kv0 = 1024
    col = lambda k: w_in[:, :, kv0 + k * LANES:kv0 + (k + 1) * LANES]
    w_nat = jnp.concatenate([w_in[:, :, 0:512], col(0), col(1), col(2), col(4), w_in[:, :, 1816:2840]], axis=-1)
    qw = w_in[:, :, 512:1024].reshape(DEPTH, D_MODEL, NSA_GROUPS, NSA_HPG, HEAD_DIM)
    qw = qw.transpose(0, 1, 3, 2, 4).reshape(DEPTH, D_MODEL, 512)
    qw = qw * (ATTN_SCALE * LOG2_E)
    gw = jnp.pad(w_in[:, :, 1792:1816], ((0, 0), (0, 0), (0, LANES - 24)))
    w_t = jnp.concatenate([qw, col(3), col(5), gw], axis=-1).transpose(0, 2, 1)
    return w_nat.astype(BF16), w_t.astype(BF16)


def _pack_cmp_weights(pe, w1, w2):
    d = pe.shape[0]
    eye = jnp.eye(NSA_GROUPS, dtype=F32)
    pe2 = jnp.broadcast_to(pe.reshape(d, 2, 16, 1, HEAD_DIM), (d, 2, 16, NSA_GROUPS, HEAD_DIM)).reshape(d, 2, 2048)
    w1r = w1.reshape(d, 2, 16, HEAD_DIM, NSA_CMP_HIDDEN)
    w1e = jnp.einsum('dplkh,ge->dplgkeh', w1r, eye).reshape(d, 2, 2048, NSA_GROUPS * NSA_CMP_HIDDEN)
    w2e = jnp.einsum('dhk,ge->dghek', w2, eye).reshape(d, NSA_GROUPS * NSA_CMP_HIDDEN, LANES)
    return pe2, w1e[:, 0].astype(BF16), w1e[:, 1].astype(BF16), w2e.astype(BF16), w2e.transpose(0, 2, 1).astype(BF16)


def _pack_out_weights(w_out):
    wc = w_out[:, 0:CONV_CH]
    wn = w_out[:, CONV_CH:CONV_CH + 512].reshape(DEPTH, NSA_GROUPS, NSA_HPG, HEAD_DIM, D_MODEL)
    wn = wn.transpose(0, 2, 1, 3, 4).reshape(DEPTH, 512, D_MODEL)
    wh = w_out[:, CONV_CH + 512:]
    return wc.astype(BF16), wn.astype(BF16), wh.astype(BF16)


def _pack_ffn_weights(w_up, conv_w, conv_b, w_down):
    cw = jnp.pad(conv_w, ((0, 0), (0, SUBLANES - conv_w.shape[1]), (0, 0)))
    return w_up.astype(BF16), cw, conv_b[:, None, :], w_down.astype(BF16)


def kernel(x, p, w_in, conv_w, conv_b, conv_ln_g, conv_ln_b, cmp_pe_k, cmp_pe_v, cmp_w1_k, cmp_w2_k, cmp_w1_v,
           cmp_w2_v, lb_logits, hgrn_norm_g, w_out, ln1_g, ln1_b, w_up, ffn_conv_w, ffn_conv_b, w_down,
           w_ple_gate, w_ple_proj, ln2_g, ln2_b):
    bsz, s, _ = x.shape
    t = bsz * s
    w_nat, w_t = _pack_in_weights(w_in)
    pe_k, w1k_t, w1k_b, w2k, _ = _pack_cmp_weights(cmp_pe_k, cmp_w1_k, cmp_w2_k)
    pe_v, w1v_t, w1v_b, _, w2v_t = _pack_cmp_weights(cmp_pe_v, cmp_w1_v, cmp_w2_v)
    wc, wn, wh = _pack_out_weights(w_out)
    wup, fcw, fcb, wdn = _pack_ffn_weights(w_up, ffn_conv_w, ffn_conv_b, w_down)
    conv_w_p = jnp.pad(conv_w, ((0, 0), (0, 1), (0, 0)))
    norm_g4 = jnp.tile(hgrn_norm_g, (1, HGRN_HEADS))[:, None, :]
    wpg = w_ple_gate.astype(BF16)
    wpp = w_ple_proj.astype(BF16)
    mt = _nsa_consts(s)
    hgrn_consts = _hgrn_consts()
    n_rows = s // NSA_CMP_STRIDE

    h = x.reshape(t, D_MODEL)
    for i in range(DEPTH):
        conv_u, kcu, vcu, kk, hg, qt, vt, gt = _inproj(h, w_nat[i], w_t[i])
        y_conv = _convmod(conv_u.reshape(bsz, s, 2 * CONV_CH), conv_w_p[i], conv_b[i][None], conv_ln_g[i][None],
                          conv_ln_b[i][None])
        kc = _compress(kcu.reshape(bsz, n_rows, NSA_CMP_STRIDE * LANES), pe_k[i], w1k_t[i], w1k_b[i], w2k[i], False)
        vct = _compress(vcu.reshape(bsz, n_rows, NSA_CMP_STRIDE * LANES), pe_v[i], w1v_t[i], w1v_b[i], w2v_t[i], True)
        y_nsa = _nsa_attention(qt, kk.reshape(bsz, s, 2 * LANES), vt, gt, kc, vct, mt, bsz, s)
        y_hgrn = _hgrn(hg.reshape(bsz, s, 4 * HGRN_W), lb_logits, norm_g4[i], hgrn_consts, i)
        h1 = _outproj(h, y_conv.reshape(t, CONV_CH), y_nsa.reshape(t, 512), y_hgrn.reshape(t, HGRN_W),
                      wc[i], wn[i], wh[i], ln1_g[i][None], ln1_b[i][None])
        h = _ffn(h1, p[i].reshape(t, D_PLE), wup[i], fcw[i], fcb[i], wdn[i], wpg[i], wpp[i],
                 ln2_g[i][None], ln2_b[i][None], s)
    return h.reshape(bsz, s, D_MODEL)
```
